```python
import math
import jax, jax.numpy as jnp
from jax import lax
import numpy as np

D_MODEL = 1024
BATCH = 8
SEQ = 4096
DEPTH = 1

N_MEM = 256
EPS = 1e-6
NEG = -1e30
D_FF = 2816
DA_HEADS = 4
DA_QK_DIM = 64
DA_V_DIM = 2 * DA_QK_DIM
NSA_HEADS = 8
NSA_KV_GROUPS = 2
NSA_REP = NSA_HEADS // NSA_KV_GROUPS
NSA_HEAD_DIM = 64
CMP_LEN = 32
CMP_STRIDE = 16
CMP_HIDDEN = 128
SEL_LEN = 64
SEL_TOPK = 16
WIN = 512
FORCE_SCORE = 1e4
XA_HEADS = 4
XA_HEAD_DIM = D_MODEL // XA_HEADS
Q_BLOCK = 128
SEL_Q_BLOCK = 64

DA_Q = DA_HEADS * 2 * DA_QK_DIM
DA_V = DA_HEADS * DA_V_DIM
NSA_Q = NSA_HEADS * NSA_HEAD_DIM
NSA_KV = NSA_KV_GROUPS * NSA_HEAD_DIM
NSA_GATE = NSA_HEADS * 3
MIX_SIZES = (DA_Q, DA_Q, DA_V, NSA_Q, NSA_KV, NSA_KV, NSA_KV, NSA_KV, NSA_KV, NSA_KV, NSA_GATE)
MIX_IN = sum(MIX_SIZES)
MIX_OUT = DA_V + NSA_Q

kernel_name = 'hymba_diff_nsa_macaron_alibi_mem'


def rms_norm(x, g):
    xf = x.astype(jnp.float32)
    y = xf * lax.rsqrt(jnp.mean(xf * xf, axis=-1, keepdims=True) + EPS)
    return (y * g.astype(jnp.float32)).astype(x.dtype)


def masked_softmax(s, mask):
    p = jax.nn.softmax(jnp.where(mask, s, NEG), axis=-1)
    return jnp.where(mask, p, 0.0)


def alibi_slopes(n):
    return jnp.asarray(2.0 ** (-8.0 * np.arange(1, n + 1) / n), dtype=jnp.float32)


def swiglu(x, w_gate, w_up, w_down):
    return (jax.nn.silu(x @ w_gate) * (x @ w_up)) @ w_down


def diff_attention(q, k, v, lam, lam_init, subln_g, slopes):
    B, H, _, T, dq = q.shape
    dv = v.shape[-1]
    nq = T // Q_BLOCK
    scale = dq ** -0.5
    qb = q.reshape(B, H, 2, nq, Q_BLOCK, dq).transpose(3, 0, 1, 2, 4, 5)
    kpos = jnp.arange(T)

    def block(args):
        qi, n = args
        qpos = n * Q_BLOCK + jnp.arange(Q_BLOCK)
        dist = qpos[:, None] - kpos[None, :]
        mask = dist >= 0
        bias = -slopes[:, None, None] * dist.astype(jnp.float32)
        s = jnp.einsum('bhmqd,bhmkd->bhmqk', qi, k,
                       preferred_element_type=jnp.float32) * scale + bias[None, :, None]
        p = masked_softmax(s, mask)
        a = p[:, :, 0] - lam * p[:, :, 1]
        return jnp.einsum('bhqk,bhkd->bhqd', a.astype(v.dtype), v)

    o = lax.map(block, (qb, jnp.arange(nq)))
    o = o.transpose(1, 2, 0, 3, 4).reshape(B, H, T, dv)
    return rms_norm(o, subln_g) * (1.0 - lam_init)


def compress(kv, pe, w1, w2):
    B, G, T, d = kv.shape
    r = CMP_LEN // CMP_STRIDE
    nch = T // CMP_STRIDE
    nc = nch - r + 1
    chunks = kv.reshape(B, G, nch, CMP_STRIDE, d)
    blocks = jnp.concatenate([chunks[:, :, i:i + nc] for i in range(r)], axis=3)
    blocks = (blocks + pe).reshape(B, G, nc, CMP_LEN * d)
    return jax.nn.silu(blocks @ w1) @ w2


def nsa_attention(q, k_c, v_c, k_s, v_s, k_w, v_w, gates,
                  cmp_k_pe, cmp_k_w1, cmp_k_w2, cmp_v_pe, cmp_v_w1, cmp_v_w2, slopes):
    B, G, R, T, d = q.shape
    scale = d ** -0.5
    f32 = jnp.float32
    tpos = jnp.arange(T)
    sl = slopes[None, :, :, None, None]

    kc = compress(k_c, cmp_k_pe, cmp_k_w1, cmp_k_w2)
    vc = compress(v_c, cmp_v_pe, cmp_v_w1, cmp_v_w2)
    nc = kc.shape[2]
    c_start = jnp.arange(nc) * CMP_STRIDE
    c_end = c_start + CMP_LEN - 1
    dist_c = tpos[:, None] - c_end[None, :]
    s_c = jnp.einsum('bgrtd,bgcd->bgrtc', q, kc, preferred_element_type=f32) * scale \
        - sl * dist_c.astype(f32)
    p_c = masked_softmax(s_c, dist_c >= 0)
    o_cmp = jnp.einsum('bgrtc,bgcd->bgrtd', p_c.astype(vc.dtype), vc)

    nsel = T // SEL_LEN
    s_start = jnp.arange(nsel) * SEL_LEN
    overlap = jnp.clip(jnp.minimum(c_start[:, None] + CMP_LEN, s_start[None, :] + SEL_LEN)
                       - jnp.maximum(c_start[:, None], s_start[None, :]), 0, None)
    m_cs = overlap.astype(f32) / CMP_LEN
    imp = jnp.einsum('bgrtc,cj->bgtj', p_c, m_cs)
    blk = jnp.arange(nsel)[None, :]
    cur = (tpos // SEL_LEN)[:, None]
    forced = (blk == 0) | (blk == cur) | (blk == cur - 1)
    score = jnp.where(forced, FORCE_SCORE, jnp.where(blk <= cur, imp, -1.0))
    topk = min(SEL_TOPK, nsel)
    _, idx = lax.top_k(score, topk)

    ks_blocks = k_s.reshape(B, G, nsel, SEL_LEN, d)
    vs_blocks = v_s.reshape(B, G, nsel, SEL_LEN, d)
    nqb = T // SEL_Q_BLOCK
    q_sel = q.reshape(B, G, R, nqb, SEL_Q_BLOCK, d).transpose(3, 0, 1, 2, 4, 5)
    i_sel = idx.reshape(B, G, nqb, SEL_Q_BLOCK, topk).transpose(2, 0, 1, 3, 4)
    bi = jnp.arange(B)[:, None, None, None]
    gi = jnp.arange(G)[None, :, None, None]

    def sel_block(args):
        qi, ii, n = args
        kg = ks_blocks[bi, gi, ii]
        vg = vs_blocks[bi, gi, ii]
        qpos = n * SEL_Q_BLOCK + jnp.arange(SEL_Q_BLOCK)
        kpos = ii[..., None] * SEL_LEN + jnp.arange(SEL_LEN)
        dist = qpos[:, None, None] - kpos
        s = jnp.einsum('bgrqd,bgqkld->bgrqkl', qi, kg, preferred_element_type=f32) * scale \
            - slopes[None, :, :, None, None, None] * dist[:, :, None].astype(f32)
        s = s.reshape(B, G, R, SEL_Q_BLOCK, topk * SEL_LEN)
        mask = (dist >= 0).reshape(B, G, 1, SEL_Q_BLOCK, topk * SEL_LEN)
        p = masked_softmax(s, mask).reshape(B, G, R, SEL_Q_BLOCK, topk, SEL_LEN)
        return jnp.einsum('bgrqkl,bgqkld->bgrqd', p.astype(vg.dtype), vg)

    o_sel = lax.map(sel_block, (q_sel, i_sel, jnp.arange(nqb)))
    o_sel = o_sel.transpose(1, 2, 3, 0, 4, 5).reshape(B, G, R, T, d)

    nwb = T // Q_BLOCK
    nback = WIN // Q_BLOCK
    wlen = (nback + 1) * Q_BLOCK
    def band(t):
        tp = jnp.pad(t, ((0, 0), (0, 0), (WIN, 0), (0, 0))).reshape(B, G, nwb + nback, Q_BLOCK, d)
        tw = jnp.concatenate([tp[:, :, i:i + nwb] for i in range(nback + 1)], axis=3)
        return tw.transpose(2, 0, 1, 3, 4)
    kwin = band(k_w)
    vwin = band(v_w)
    q_win = q.reshape(B, G, R, nwb, Q_BLOCK, d).transpose(3, 0, 1, 2, 4, 5)

    def win_block(args):
        qi, ki, vi, n = args
        qpos = n * Q_BLOCK + jnp.arange(Q_BLOCK)
        kpos = n * Q_BLOCK - WIN + jnp.arange(wlen)
        dist = qpos[:, None] - kpos[None, :]
        mask = (dist >= 0) & (dist < WIN) & (kpos[None, :] >= 0)
        s = jnp.einsum('bgrqd,bgkd->bgrqk', qi, ki, preferred_element_type=f32) * scale \
            - sl * dist.astype(f32)
        p = masked_softmax(s, mask)
        return jnp.einsum('bgrqk,bgkd->bgrqd', p.astype(vi.dtype), vi)

    o_win = lax.map(win_block, (q_win, kwin, vwin, jnp.arange(nwb)))
    o_win = o_win.transpose(1, 2, 3, 0, 4, 5).reshape(B, G, R, T, d)

    return (gates[..., 0:1] * o_cmp + gates[..., 1:2] * o_sel + gates[..., 2:3] * o_win)


def mem_cross_attention(h, m, w_q, w_k, w_v, w_o):
    B, T, _ = h.shape
    M = m.shape[1]
    q = (h @ w_q).reshape(B, T, XA_HEADS, XA_HEAD_DIM)
    k = (m @ w_k).reshape(B, M, XA_HEADS, XA_HEAD_DIM)
    v = (m @ w_v).reshape(B, M, XA_HEADS, XA_HEAD_DIM)
    s = jnp.einsum('bthd,bmhd->bhtm', q, k, preferred_element_type=jnp.float32) * XA_HEAD_DIM ** -0.5
    p = jax.nn.softmax(s, axis=-1)
    o = jnp.einsum('bhtm,bmhd->bthd', p.astype(v.dtype), v)
    return o.reshape(B, T, XA_HEADS * XA_HEAD_DIM) @ w_o


def setup_inputs(seed: int = 0) -> dict:
    key = jax.random.key(seed)
    keys = jax.random.split(key, 40)
    kit = (keys[i] for i in range(40))
    L = DEPTH

    def nrm(shape, scale):
        return jax.random.normal(next(kit), shape, jnp.float32) * scale

    def gain(n):
        return 1.0 + nrm((L, n), 0.05)

    d = D_MODEL
    inp = {}
    inp['x'] = nrm((BATCH, SEQ, d), 1.0)
    inp['mem'] = nrm((BATCH, N_MEM, d), 1.0)
    inp['ffn1_pre_g'] = gain(d)
    inp['ffn1_post_g'] = gain(d)
    inp['ffn1_w_gate'] = nrm((L, d, D_FF), d ** -0.5)
    inp['ffn1_w_up'] = nrm((L, d, D_FF), d ** -0.5)
    inp['ffn1_w_down'] = nrm((L, D_FF, d), D_FF ** -0.5)
    inp['mix_pre_g'] = gain(d)
    inp['mix_post_g'] = gain(d)
    inp['w_mix_in'] = nrm((L, d, MIX_IN), d ** -0.5)
    inp['da_lambda_q1'] = nrm((L, DA_QK_DIM), 0.1)
    inp['da_lambda_k1'] = nrm((L, DA_QK_DIM), 0.1)
    inp['da_lambda_q2'] = nrm((L, DA_QK_DIM), 0.1)
    inp['da_lambda_k2'] = nrm((L, DA_QK_DIM), 0.1)
    inp['da_subln_g'] = gain(DA_V_DIM)
    inp['cmp_k_pe'] = nrm((L, CMP_LEN, NSA_HEAD_DIM), 0.02)
    inp['cmp_k_w1'] = nrm((L, CMP_LEN * NSA_HEAD_DIM, CMP_HIDDEN), (CMP_LEN * NSA_HEAD_DIM) ** -0.5)
    inp['cmp_k_w2'] = nrm((L, CMP_HIDDEN, NSA_HEAD_DIM), CMP_HIDDEN ** -0.5)
    inp['cmp_v_pe'] = nrm((L, CMP_LEN, NSA_HEAD_DIM), 0.02)
    inp['cmp_v_w1'] = nrm((L, CMP_LEN * NSA_HEAD_DIM, CMP_HIDDEN), (CMP_LEN * NSA_HEAD_DIM) ** -0.5)
    inp['cmp_v_w2'] = nrm((L, CMP_HIDDEN, NSA_HEAD_DIM), CMP_HIDDEN ** -0.5)
    inp['w_mix_out'] = nrm((L, MIX_OUT, d), MIX_OUT ** -0.5)
    inp['xa_pre_g'] = gain(d)
    inp['xa_post_g'] = gain(d)
    inp['mem_norm_g'] = gain(d)
    inp['xa_w_q'] = nrm((L, d, XA_HEADS * XA_HEAD_DIM), d ** -0.5)
    inp['xa_w_k'] = nrm((L, d, XA_HEADS * XA_HEAD_DIM), d ** -0.5)
    inp['xa_w_v'] = nrm((L, d, XA_HEADS * XA_HEAD_DIM), d ** -0.5)
    inp['xa_w_o'] = nrm((L, XA_HEADS * XA_HEAD_DIM, d), (XA_HEADS * XA_HEAD_DIM) ** -0.5)
    inp['ffn2_pre_g'] = gain(d)
    inp['ffn2_post_g'] = gain(d)
    inp['ffn2_w_gate'] = nrm((L, d, D_FF), d ** -0.5)
    inp['ffn2_w_up'] = nrm((L, d, D_FF), d ** -0.5)
    inp['ffn2_w_down'] = nrm((L, D_FF, d), D_FF ** -0.5)
    return inp


def reference(x, mem, ffn1_pre_g, ffn1_post_g, ffn1_w_gate, ffn1_w_up, ffn1_w_down,
              mix_pre_g, mix_post_g, w_mix_in, da_lambda_q1, da_lambda_k1, da_lambda_q2,
              da_lambda_k2, da_subln_g, cmp_k_pe, cmp_k_w1, cmp_k_w2, cmp_v_pe, cmp_v_w1,
              cmp_v_w2, w_mix_out, xa_pre_g, xa_post_g, mem_norm_g, xa_w_q, xa_w_k, xa_w_v,
              xa_w_o, ffn2_pre_g, ffn2_post_g, ffn2_w_gate, ffn2_w_up, ffn2_w_down):
    B, T, _ = x.shape
    split_idx = [int(v) for v in np.cumsum(MIX_SIZES)[:-1]]
    da_slopes = alibi_slopes(DA_HEADS)
    nsa_slopes = alibi_slopes(NSA_HEADS).reshape(NSA_KV_GROUPS, NSA_REP)
    for l in range(DEPTH):
        h = rms_norm(x, ffn1_pre_g[l])
        x = x + 0.5 * rms_norm(swiglu(h, ffn1_w_gate[l], ffn1_w_up[l], ffn1_w_down[l]), ffn1_post_g[l])

        h = rms_norm(x, mix_pre_g[l])
        z = h @ w_mix_in[l]
        da_q, da_k, da_v, n_q, n_kc, n_vc, n_ks, n_vs, n_kw, n_vw, n_g = jnp.split(z, split_idx, axis=-1)

        lam_init = 0.8 - 0.6 * math.exp(-0.3 * l)
        lam = (jnp.exp(jnp.sum(da_lambda_q1[l] * da_lambda_k1[l]))
               - jnp.exp(jnp.sum(da_lambda_q2[l] * da_lambda_k2[l])) + lam_init)
        q_a = da_q.reshape(B, T, DA_HEADS, 2, DA_QK_DIM).transpose(0, 2, 3, 1, 4)
        k_a = da_k.reshape(B, T, DA_HEADS, 2, DA_QK_DIM).transpose(0, 2, 3, 1, 4)
        v_a = da_v.reshape(B, T, DA_HEADS, DA_V_DIM).transpose(0, 2, 1, 3)
        o_a = diff_attention(q_a, k_a, v_a, lam, lam_init, da_subln_g[l], da_slopes)
        o_a = o_a.transpose(0, 2, 1, 3).reshape(B, T, DA_V)

        def kvh(t):
            return t.reshape(B, T, NSA_KV_GROUPS, NSA_HEAD_DIM).transpose(0, 2, 1, 3)
        q_b = n_q.reshape(B, T, NSA_KV_GROUPS, NSA_REP, NSA_HEAD_DIM).transpose(0, 2, 3, 1, 4)
        gates = jax.nn.sigmoid(n_g.reshape(B, T, NSA_KV_GROUPS, NSA_REP, 3)).transpose(0, 2, 3, 1, 4)
        o_b = nsa_attention(q_b, kvh(n_kc), kvh(n_vc), kvh(n_ks), kvh(n_vs), kvh(n_kw), kvh(n_vw),
                            gates, cmp_k_pe[l], cmp_k_w1[l], cmp_k_w2[l],
                            cmp_v_pe[l], cmp_v_w1[l], cmp_v_w2[l], nsa_slopes)
        o_b = o_b.transpose(0, 3, 1, 2, 4).reshape(B, T, NSA_Q)

        o = jnp.concatenate([o_a, o_b], axis=-1) @ w_mix_out[l]
        x = x + rms_norm(o, mix_post_g[l])

        h = rms_norm(x, xa_pre_g[l])
        m = rms_norm(mem, mem_norm_g[l])
        x = x + rms_norm(mem_cross_attention(h, m, xa_w_q[l], xa_w_k[l], xa_w_v[l], xa_w_o[l]), xa_post_g[l])

        h = rms_norm(x, ffn2_pre_g[l])
        x = x + 0.5 * rms_norm(swiglu(h, ffn2_w_gate[l], ffn2_w_up[l], ffn2_w_down[l]), ffn2_post_g[l])
    return x
```

```python
import functools

import numpy as np
import jax
import jax.numpy as jnp
from jax import lax
from jax.experimental import pallas as pl
from jax.experimental.pallas import tpu as pltpu

BF = jnp.bfloat16
F32 = jnp.float32

EPS = 1e-6
NEG = -1e30
LANES = 128
HALF = LANES // 2

DA_HEADS = 4
DA_QK_DIM = 64
DA_V_DIM = 128
NSA_GROUPS = 2
NSA_REP = 4
NSA_HEAD_DIM = 64
CMP_LEN = 32
CMP_STRIDE = 16
SEL_LEN = 64
SEL_TOPK = 16
WIN = 512
FORCE_SCORE = 1e4
XA_HEADS = 4
SEL_MASK_BIG = 32768.0

VMEM_LIMIT = 56 * 1024 * 1024

_NT = (((1,), (1,)), ((), ()))


def _rms(x, g):
    ms = jnp.mean(x * x, axis=-1, keepdims=True)
    return x * lax.rsqrt(ms + EPS) * g


def _dot(a, b):
    return jnp.dot(a, b, preferred_element_type=F32)


def _dot_nt(a, b):
    return lax.dot_general(a, b, _NT, preferred_element_type=F32)


def _const_spec(shape):
    nd = len(shape)
    return pl.BlockSpec(shape, lambda *_: (0,) * nd, pipeline_mode=pl.Buffered(1))


def _alibi_slopes(n):
    return [2.0 ** (-8.0 * (i + 1) / n) for i in range(n)]


def _ffn_kernel(x_ref, pre_ref, post_ref, wg_ref, wu_ref, wd_ref, o_ref):
    x = x_ref[...]
    h = _rms(x, pre_ref[...]).astype(BF)
    g = _dot(h, wg_ref[...])
    u = _dot(h, wu_ref[...])
    a = (g * jax.nn.sigmoid(g) * u).astype(BF)
    y = _dot(a, wd_ref[...])
    o_ref[...] = x + 0.5 * _rms(y, post_ref[...])


def _ffn(x2, pre_g, post_g, wg, wu, wd, *, tm):
    n, d = x2.shape
    f = wg.shape[1]
    return pl.pallas_call(
        _ffn_kernel,
        out_shape=jax.ShapeDtypeStruct((n, d), F32),
        grid=(n // tm,),
        in_specs=[
            pl.BlockSpec((tm, d), lambda i: (i, 0)),
            _const_spec((1, d)), _const_spec((1, d)),
            _const_spec((d, f)), _const_spec((d, f)), _const_spec((f, d)),
        ],
        out_specs=pl.BlockSpec((tm, d), lambda i: (i, 0)),
        compiler_params=pltpu.CompilerParams(
            dimension_semantics=("arbitrary",), vmem_limit_bytes=VMEM_LIMIT),
        name="ffn",
    )(x2, pre_g, post_g, wg, wu, wd)


def _mixin_kernel(x_ref, g_ref, w_ref, qfda_ref, qfn_ref,
                  daq, dak, dav, nq, nks, nkw, nvs, nvw, nkc, nvc, gat):
    tm = x_ref.shape[0]
    t0 = pl.program_id(1) * tm
    h = _rms(x_ref[...], g_ref[...]).astype(BF)
    z = _dot(h, w_ref[...])

    lane = lax.broadcasted_iota(jnp.int32, (tm, LANES), 1)
    row = lax.broadcasted_iota(jnp.int32, (tm, LANES), 0) + t0
    pa = (row >> 6).astype(F32)
    pb = (row & 63).astype(F32)
    low = lane < HALF
    pos_lo = jnp.where(lane == HALF, pa, jnp.where(lane == HALF + 1, pb, 0.0))
    pos_hi = jnp.where(lane == 0, pa, jnp.where(lane == 1, pb, 0.0))
    one_lo = jnp.where(lane == HALF, 1.0, 0.0)
    one_hi = jnp.where(lane == 0, 1.0, 0.0)

    def ch(i):
        return z[:, i * LANES:(i + 1) * LANES]

    def put(ref, j, val):
        ref[:, j * LANES:(j + 1) * LANES] = val.astype(ref.dtype)

    for c in range(4):
        q = ch(c) * 0.125
        put(daq, 2 * c, jnp.where(low, q, qfda_ref[:, (2 * c) * LANES:(2 * c + 1) * LANES]))
        put(daq, 2 * c + 1, jnp.where(low, qfda_ref[:, (2 * c + 1) * LANES:(2 * c + 2) * LANES], q))
        k = ch(4 + c)
        put(dak, 2 * c, jnp.where(low, k, pos_lo))
        put(dak, 2 * c + 1, jnp.where(low, pos_hi, k))
        put(dav, c, ch(8 + c))
        q = ch(12 + c) * 0.125
        put(nq, 2 * c, jnp.where(low, q, qfn_ref[:, (2 * c) * LANES:(2 * c + 1) * LANES]))
        put(nq, 2 * c + 1, jnp.where(low, qfn_ref[:, (2 * c + 1) * LANES:(2 * c + 2) * LANES], q))
    for src, ref in ((16, nks), (17, nkw)):
        k = ch(src)
        put(ref, 0, jnp.where(low, k, pos_lo))
        put(ref, 1, jnp.where(low, pos_hi, k))
    for src, ref in ((18, nvs), (19, nvw)):
        v = ch(src)
        put(ref, 0, jnp.where(low, v, one_lo))
        put(ref, 1, jnp.where(low, one_hi, v))
    nkc[...] = ch(20)
    nvc[...] = ch(21)
    gat[...] = jax.nn.sigmoid(ch(22))


def _mixin(x, g, w, qf_da, qf_n, *, tm):
    b, t, d = x.shape
    wcols = w.shape[1]
    widths = (1024, 1024, 512, 1024, 256, 256, 256, 256, 128, 128, 128)
    dtypes = (BF, BF, BF, BF, BF, BF, BF, BF, F32, F32, F32)
    return pl.pallas_call(
        _mixin_kernel,
        out_shape=[jax.ShapeDtypeStruct((b, t, wd), dt) for wd, dt in zip(widths, dtypes)],
        grid=(b, t // tm),
        in_specs=[
            pl.BlockSpec((None, tm, d), lambda i, j: (i, j, 0)),
            _const_spec((1, d)), _const_spec((d, wcols)),
            _const_spec((1, 1024)), _const_spec((1, 1024)),
        ],
        out_specs=[pl.BlockSpec((None, tm, wd), lambda i, j: (i, j, 0)) for wd in widths],
        compiler_params=pltpu.CompilerParams(
            dimension_semantics=("arbitrary", "arbitrary"), vmem_limit_bytes=VMEM_LIMIT),
        name="mixin",
    )(x, g, w, qf_da, qf_n)


def _da_kernel(q_ref, k_ref, v_ref, lq1, lk1, lq2, lk2, sg_ref, o_ref, acc_ref, m_ref, l_ref,
               *, lam_init, tq):
    qi = pl.program_id(2)
    q = q_ref[...]
    m_ref[...] = jnp.full(m_ref.shape, NEG, F32)
    l_ref[...] = jnp.zeros(l_ref.shape, F32)
    acc_ref[...] = jnp.zeros(acc_ref.shape, F32)
    rowi = lax.broadcasted_iota(jnp.int32, (tq, tq), 0)
    coli = lax.broadcasted_iota(jnp.int32, (tq, tq), 1)

    def step(j, masked):
        ks = pl.multiple_of(j * tq, tq)
        kt = k_ref[pl.ds(ks, tq), :]
        vt = v_ref[pl.ds(ks, tq), :]
        for mp in range(2):
            s = _dot_nt(q[:, mp * LANES:(mp + 1) * LANES], kt[:, mp * LANES:(mp + 1) * LANES])
            if masked:
                s = jnp.where(coli <= rowi, s, NEG)
            m_old = m_ref[mp]
            m_new = jnp.maximum(m_old, jnp.max(s, axis=1, keepdims=True))
            alpha = jnp.exp(m_old - m_new)
            p = jnp.exp(s - m_new)
            l_ref[mp] = alpha * l_ref[mp] + jnp.sum(p, axis=1, keepdims=True)
            acc_ref[mp] = alpha * acc_ref[mp] + _dot(p.astype(BF), vt)
            m_ref[mp] = m_new

    def body(j, c):
        step(j, False)
        return c

    lax.fori_loop(0, qi, body, 0)
    step(qi, True)

    lam = (jnp.exp(jnp.sum(lq1[...] * lk1[...], axis=1, keepdims=True))
           - jnp.exp(jnp.sum(lq2[...] * lk2[...], axis=1, keepdims=True)) + lam_init)
    o = acc_ref[0] / l_ref[0] - lam * (acc_ref[1] / l_ref[1])
    o_ref[...] = _rms(o, sg_ref[...]) * (1.0 - lam_init)


def _da_attention(daq, dak, dav, lq1, lk1, lq2, lk2, subln_g, *, lam_init, tq):
    b, t, _ = daq.shape
    vec = _const_spec((1, DA_QK_DIM))
    return pl.pallas_call(
        functools.partial(_da_kernel, lam_init=lam_init, tq=tq),
        out_shape=jax.ShapeDtypeStruct((b, t, DA_HEADS * DA_V_DIM), F32),
        grid=(b, DA_HEADS, t // tq),
        in_specs=[
            pl.BlockSpec((None, tq, 2 * LANES), lambda i, h, j: (i, j, h)),
            pl.BlockSpec((None, t, 2 * LANES), lambda i, h, j: (i, 0, h)),
            pl.BlockSpec((None, t, DA_V_DIM), lambda i, h, j: (i, 0, h)),
            vec, vec, vec, vec, _const_spec((1, DA_V_DIM)),
        ],
        out_specs=pl.BlockSpec((None, tq, DA_V_DIM), lambda i, h, j: (i, j, h)),
        scratch_shapes=[
            pltpu.VMEM((2, tq, DA_V_DIM), F32),
            pltpu.VMEM((2, tq, 1), F32),
            pltpu.VMEM((2, tq, 1), F32),
        ],
        compiler_params=pltpu.CompilerParams(
            dimension_semantics=("arbitrary", "arbitrary", "arbitrary"), vmem_limit_bytes=VMEM_LIMIT),
        name="diff_attn",
    )(daq, dak, dav, lq1, lk1, lq2, lk2, subln_g)


def _compress_kernel(xk_ref, xv_ref, pek_ref, pev_ref, w1k_ref, w1v_ref, w2k_ref, w2v_ref, cpos_ref,
                     kc_ref, vct_ref):
    ncp = xk_ref.shape[0]
    for x_ref, pe_ref, w1_ref, w2_ref, is_v in ((xk_ref, pek_ref, w1k_ref, w2k_ref, False),
                                                (xv_ref, pev_ref, w1v_ref, w2v_ref, True)):
        x = x_ref[...]
        xa = (x + pe_ref[0]).astype(BF)
        xb = (x + pe_ref[1]).astype(BF)
        for g in range(NSA_GROUPS):
            first = _dot(xa, w1_ref[0, g])
            second = _dot(xb, w1_ref[1, g])
            hid = first + pltpu.roll(second, ncp - 1, 0)
            act = (hid * jax.nn.sigmoid(hid)).astype(BF)
            out = _dot(act, w2_ref[g])
            if is_v:
                vct_ref[g] = out.T.astype(BF)
            else:
                kc_ref[g] = (out + cpos_ref[g]).astype(BF)


def _compress(xk, xv, pek, pev, w1k, w1v, w2k, w2v, cpos):
    b, ncp, width = xk.shape
    x_spec = pl.BlockSpec((None, ncp, width), lambda i: (i, 0, 0))
    return pl.pallas_call(
        _compress_kernel,
        out_shape=[jax.ShapeDtypeStruct((b, NSA_GROUPS, ncp, LANES), BF),
                   jax.ShapeDtypeStruct((b, NSA_GROUPS, LANES, ncp), BF)],
        grid=(b,),
        in_specs=[x_spec, x_spec,
                  _const_spec(pek.shape), _const_spec(pev.shape),
                  _const_spec(w1k.shape), _const_spec(w1v.shape),
                  _const_spec(w2k.shape), _const_spec(w2v.shape), _const_spec(cpos.shape)],
        out_specs=[pl.BlockSpec((None, NSA_GROUPS, ncp, LANES), lambda i: (i, 0, 0, 0)),
                   pl.BlockSpec((None, NSA_GROUPS, LANES, ncp), lambda i: (i, 0, 0, 0))],
        compiler_params=pltpu.CompilerParams(
            dimension_semantics=("arbitrary",), vmem_limit_bytes=VMEM_LIMIT),
        name="compress",
    )(xk, xv, pek, pev, w1k, w1v, w2k, w2v, cpos)


def _cmp_kernel(nq_ref, kc_ref, vct_ref, mt_ref, o_ref, sf_ref, ob_ref, *, tq, topk):
    ncp = kc_ref.shape[1]
    ns = mt_ref.shape[0]
    t0 = pl.program_id(1) * tq
    cidx = lax.broadcasted_iota(jnp.int32, (ncp, tq), 0)
    tpos = lax.broadcasted_iota(jnp.int32, (ncp, tq), 1) + t0
    valid = (cidx * CMP_STRIDE + (CMP_LEN - 1)) <= tpos
    lane = lax.broadcasted_iota(jnp.int32, (tq, LANES), 1)
    mt = mt_ref[...]

    blk = lax.broadcasted_iota(jnp.int32, (ns, tq), 0)
    cur = (lax.broadcasted_iota(jnp.int32, (ns, tq), 1) + t0) >> 6
    sub8 = lax.broadcasted_iota(jnp.int32, (8, tq), 0)

    for g in range(NSA_GROUPS):
        kc = kc_ref[g]
        vct = vct_ref[g]
        imp = jnp.zeros((ns, tq), F32)
        for r in range(NSA_REP):
            c = 2 * r + g
            q = nq_ref[:, c * LANES:(c + 1) * LANES]
            s = jnp.where(valid, _dot_nt(kc, q), NEG)
            m = jnp.max(s, axis=0, keepdims=True)
            e = jnp.exp(s - m)
            p = jnp.where(valid, e / jnp.sum(e, axis=0, keepdims=True), 0.0)
            p_hi = p.astype(BF)
            p_lo = (p - p_hi.astype(F32)).astype(BF)
            imp = imp + _dot(mt, p_hi) + _dot(mt, p_lo)
            o = _dot(vct, p_hi).T
            if g == 0:
                ob_ref[r] = o
            else:
                o_ref[:, r * LANES:(r + 1) * LANES] = jnp.where(lane < HALF, ob_ref[r], o)

        score = jnp.where(blk == 0, FORCE_SCORE,
                          jnp.where(blk == cur, FORCE_SCORE,
                                    jnp.where(blk == cur - 1, FORCE_SCORE,
                                              jnp.where(blk <= cur, imp, -1.0))))
        rows = [jnp.broadcast_to(score[i:i + 1, :], (8, tq)) for i in range(ns)]
        feats = []
        for kt in range(ns // 8):
            sc = score[8 * kt:8 * kt + 8, :]
            cnt = jnp.zeros((8, tq), F32)
            for i in range(ns):
                if i < 8 * kt:
                    beat = jnp.where(rows[i] >= sc, 1.0, 0.0)
                elif i >= 8 * kt + 8:
                    beat = jnp.where(rows[i] > sc, 1.0, 0.0)
                else:
                    beat = jnp.where(sub8 + 8 * kt > i,
                                     jnp.where(rows[i] >= sc, 1.0, 0.0),
                                     jnp.where(rows[i] > sc, 1.0, 0.0))
                cnt = cnt + beat
            feats.append(jnp.where(cnt < topk, 0.0, -SEL_MASK_BIG))
        feats.append(jnp.zeros((LANES - ns, tq), F32))
        sf_ref[g] = jnp.concatenate(feats, axis=0).T.astype(BF)


def _cmp_topk(nq, kc, vct, mt, *, tq, topk):
    b, t, _ = nq.shape
    ncp = kc.shape[2]
    return pl.pallas_call(
        functools.partial(_cmp_kernel, tq=tq, topk=topk),
        out_shape=[jax.ShapeDtypeStruct((b, t, NSA_REP * LANES), F32),
                   jax.ShapeDtypeStruct((b, NSA_GROUPS, t, LANES), BF)],
        grid=(b, t // tq),
        in_specs=[
            pl.BlockSpec((None, tq, 2 * NSA_REP * LANES), lambda i, j: (i, j, 0)),
            pl.BlockSpec((None, NSA_GROUPS, ncp, LANES), lambda i, j: (i, 0, 0, 0)),
            pl.BlockSpec((None, NSA_GROUPS, LANES, ncp), lambda i, j: (i, 0, 0, 0)),
            _const_spec(mt.shape),
        ],
        out_specs=[pl.BlockSpec((None, tq, NSA_REP * LANES), lambda i, j: (i, j, 0)),
                   pl.BlockSpec((None, NSA_GROUPS, tq, LANES), lambda i, j: (i, 0, j, 0))],
        scratch_shapes=[pltpu.VMEM((NSA_REP, tq, LANES), F32)],
        compiler_params=pltpu.CompilerParams(
            dimension_semantics=("arbitrary", "arbitrary"), vmem_limit_bytes=VMEM_LIMIT),
        name="cmp_topk",
    )(nq, kc, vct, mt)


def _nsa_flash_kernel(*refs, tq, window, selected):
    if selected:
        nq_ref, sf_ref, k_ref, oh_ref, v_ref, o_ref, qs_ref, acc_ref, m_ref, ob_ref = refs
    else:
        nq_ref, k_ref, v_ref, o_ref, qs_ref, acc_ref, m_ref, ob_ref = refs
    qi = pl.program_id(1)
    rows = NSA_REP * tq
    rowi = lax.broadcasted_iota(jnp.int32, (rows, tq), 0) & (tq - 1)
    coli = lax.broadcasted_iota(jnp.int32, (rows, tq), 1)
    lane = lax.broadcasted_iota(jnp.int32, (tq, LANES), 1)

    for g in range(NSA_GROUPS):
        for r in range(NSA_REP):
            c = 2 * r + g
            qs_ref[r * tq:(r + 1) * tq, 0:LANES] = nq_ref[:, c * LANES:(c + 1) * LANES]
            if selected:
                qs_ref[r * tq:(r + 1) * tq, LANES:2 * LANES] = sf_ref[g]
        m_ref[...] = jnp.full(m_ref.shape, NEG, F32)
        acc_ref[...] = jnp.zeros(acc_ref.shape, F32)

        def step(j, mode, g=g):
            ks = pl.multiple_of(j * tq, tq)
            kt = k_ref[pl.ds(ks, tq), g * LANES:(g + 1) * LANES]
            if selected:
                s = (_dot_nt(qs_ref[:, 0:LANES], kt)
                     + _dot_nt(qs_ref[:, LANES:2 * LANES], oh_ref[pl.ds(ks, tq), :]))
            else:
                s = _dot_nt(qs_ref[...], kt)
            if mode == "diag":
                s = jnp.where(coli <= rowi, s, NEG)
            elif mode == "edge":
                s = jnp.where(coli > rowi, s, NEG)
            m_old = m_ref[...]
            m_new = jnp.maximum(m_old, jnp.max(s, axis=1, keepdims=True))
            alpha = jnp.exp(m_old - m_new)
            p = jnp.exp(s - m_new)
            acc_ref[...] = alpha * acc_ref[...] + _dot(
                p.astype(BF), v_ref[pl.ds(ks, tq), g * LANES:(g + 1) * LANES])
            m_ref[...] = m_new

        if window:
            nback = WIN // tq

            @pl.when(qi >= nback)
            def _():
                step(qi - nback, "edge")

            def body(j, carry):
                step(j, "full")
                return carry

            lax.fori_loop(jnp.maximum(qi - nback + 1, 0), qi, body, 0)
        else:
            def body(j, carry):
                step(j, "full")
                return carry

            lax.fori_loop(0, qi, body, 0)
        step(qi, "diag")

        lcol = HALF if g == 0 else 0
        for r in range(NSA_REP):
            a = acc_ref[r * tq:(r + 1) * tq, :]
            o = a / a[:, lcol:lcol + 1]
            if g == 0:
                ob_ref[r] = o
            else:
                o_ref[:, r * LANES:(r + 1) * LANES] = jnp.where(lane < HALF, ob_ref[r], o)


def _nsa_flash(nq, k, v, sf=None, oh=None, *, tq, window):
    b, t, _ = nq.shape
    selected = sf is not None
    kdim = 2 * LANES if selected else LANES
    in_specs = [pl.BlockSpec((None, tq, 2 * NSA_REP * LANES), lambda i, j: (i, j, 0))]
    args = [nq]
    if selected:
        in_specs.append(pl.BlockSpec((None, NSA_GROUPS, tq, LANES), lambda i, j: (i, 0, j, 0)))
        args.append(sf)
    in_specs.append(pl.BlockSpec((None, t, NSA_GROUPS * LANES), lambda i, j: (i, 0, 0)))
    args.append(k)
    if selected:
        in_specs.append(_const_spec(oh.shape))
        args.append(oh)
    in_specs.append(pl.BlockSpec((None, t, NSA_GROUPS * LANES), lambda i, j: (i, 0, 0)))
    args.append(v)
    return pl.pallas_call(
        functools.partial(_nsa_flash_kernel, tq=tq, window=window, selected=selected),
        out_shape=jax.ShapeDtypeStruct((b, t, NSA_REP * LANES), F32),
        grid=(b, t // tq),
        in_specs=in_specs,
        out_specs=pl.BlockSpec((None, tq, NSA_REP * LANES), lambda i, j: (i, j, 0)),
        scratch_shapes=[
            pltpu.VMEM((NSA_REP * tq, kdim), BF),
            pltpu.VMEM((NSA_REP * tq, LANES), F32),
            pltpu.VMEM((NSA_REP * tq, 1), F32),
            pltpu.VMEM((NSA_REP, tq, LANES), F32),
        ],
        compiler_params=pltpu.CompilerParams(
            dimension_semantics=("arbitrary", "arbitrary"), vmem_limit_bytes=VMEM_LIMIT),
        name="nsa_window" if window else "nsa_selected",
    )(*args)


def _mixout_kernel(x_ref, oa_ref, oc_ref, os_ref, ow_ref, gat_ref, e_ref, wa_ref, wb_ref, post_ref, o_ref):
    gt = gat_ref[...]
    g_hi = gt.astype(BF)
    g_lo = (gt - g_hi.astype(F32)).astype(BF)
    ob = None
    for c, ref in enumerate((oc_ref, os_ref, ow_ref)):
        gate = _dot(g_hi, e_ref[c]) + _dot(g_lo, e_ref[c])
        term = gate * ref[...]
        ob = term if ob is None else ob + term
    y = _dot(oa_ref[...].astype(BF), wa_ref[...]) + _dot(ob.astype(BF), wb_ref[...])
    o_ref[...] = x_ref[...] + _rms(y, post_ref[...])


def _mixout(x2, oa, oc, osel, ow, gates, e, wa, wb, post_g, *, tm):
    n, d = x2.shape
    row = lambda w: pl.BlockSpec((tm, w), lambda i: (i, 0))
    return pl.pallas_call(
        _mixout_kernel,
        out_shape=jax.ShapeDtypeStruct((n, d), F32),
        grid=(n // tm,),
        in_specs=[row(d), row(512), row(512), row(512), row(512), row(LANES),
                  _const_spec(e.shape), _const_spec(wa.shape), _const_spec(wb.shape), _const_spec((1, d))],
        out_specs=row(d),
        compiler_params=pltpu.CompilerParams(
            dimension_semantics=("arbitrary",), vmem_limit_bytes=VMEM_LIMIT),
        name="mixout",
    )(x2, oa, oc, osel, ow, gates, e, wa, wb, post_g)


def _memkv_kernel(m_ref, g_ref, wk_ref, wv_ref, k_ref, v_ref):
    m = _rms(m_ref[...], g_ref[...]).astype(BF)
    k_ref[...] = _dot(m, wk_ref[...]).astype(BF)
    v_ref[...] = _dot(m, wv_ref[...]).astype(BF)


def _memkv(mem, g, wk, wv):
    b, nm, d = mem.shape
    spec = pl.BlockSpec((None, nm, d), lambda i: (i, 0, 0))
    return pl.pallas_call(
        _memkv_kernel,
        out_shape=[jax.ShapeDtypeStruct((b, nm, d), BF)] * 2,
        grid=(b,),
        in_specs=[spec, _const_spec((1, d)), _const_spec(wk.shape), _const_spec(wv.shape)],
        out_specs=[spec, spec],
        compiler_params=pltpu.CompilerParams(
            dimension_semantics=("arbitrary",), vmem_limit_bytes=VMEM_LIMIT),
        name="mem_kv",
    )(mem, g, wk, wv)


def _xa_kernel(x_ref, pre_ref, post_ref, wq_ref, wo_ref, k_ref, v_ref, o_ref, *, scale):
    x = x_ref[...]
    h = _rms(x, pre_ref[...]).astype(BF)
    q = (_dot(h, wq_ref[...]) * scale).astype(BF)
    hd = q.shape[1] // XA_HEADS
    outs = []
    for i in range(XA_HEADS):
        sl = slice(i * hd, (i + 1) * hd)
        s = _dot_nt(q[:, sl], k_ref[:, sl])
        m = jnp.max(s, axis=1, keepdims=True)
        e = jnp.exp(s - m)
        p = e / jnp.sum(e, axis=1, keepdims=True)
        outs.append(_dot(p.astype(BF), v_ref[:, sl]).astype(BF))
    y = _dot(jnp.concatenate(outs, axis=1), wo_ref[...])
    o_ref[...] = x + _rms(y, post_ref[...])


def _xa(x, pre_g, post_g, wq, wo, k, v, *, tm):
    b, t, d = x.shape
    nm = k.shape[1]
    hd = wq.shape[1] // XA_HEADS
    row = pl.BlockSpec((None, tm, d), lambda i, j: (i, j, 0))
    kv = pl.BlockSpec((None, nm, k.shape[2]), lambda i, j: (i, 0, 0))
    return pl.pallas_call(
        functools.partial(_xa_kernel, scale=float(hd) ** -0.5),
        out_shape=jax.ShapeDtypeStruct((b, t, d), F32),
        grid=(b, t // tm),
        in_specs=[row, _const_spec((1, d)), _const_spec((1, d)),
                  _const_spec(wq.shape), _const_spec(wo.shape), kv, kv],
        out_specs=row,
        compiler_params=pltpu.CompilerParams(
            dimension_semantics=("arbitrary", "arbitrary"), vmem_limit_bytes=VMEM_LIMIT),
        name="mem_xattn",
    )(x, pre_g, post_g, wq, wo, k, v)


def _query_feature_rows(slopes_by_chunk):
    out = np.zeros((1, len(slopes_by_chunk) * LANES), np.float32)
    for j, s in enumerate(slopes_by_chunk):
        base = j * LANES + (HALF if j % 2 == 0 else 0)
        out[0, base] = s * 64.0
        out[0, base + 1] = s
    return jnp.asarray(out)


def _static_tables(t):
    ncp = t // CMP_STRIDE
    ns = t // SEL_LEN
    da = _alibi_slopes(DA_HEADS)
    qf_da = _query_feature_rows([da[j // 2] for j in range(2 * DA_HEADS)])
    nsa = _alibi_slopes(NSA_GROUPS * NSA_REP)
    qf_n = _query_feature_rows([nsa[(j % 2) * NSA_REP + j // 2] for j in range(2 * NSA_REP)])

    c_end = np.arange(ncp) * CMP_STRIDE + CMP_LEN - 1
    cpos = np.zeros((NSA_GROUPS, ncp, LANES), np.float32)
    cpos[0, :, HALF] = c_end // 64
    cpos[0, :, HALF + 1] = c_end % 64
    cpos[1, :, 0] = c_end // 64
    cpos[1, :, 1] = c_end % 64

    c_start = np.arange(ncp) * CMP_STRIDE
    s_start = np.arange(ns) * SEL_LEN
    overlap = np.clip(np.minimum(c_start[:, None] + CMP_LEN, s_start[None, :] + SEL_LEN)
                      - np.maximum(c_start[:, None], s_start[None, :]), 0, None)
    mt = (overlap.astype(np.float32) / CMP_LEN).T
    mt[:, ncp - 1] = 0.0

    oh = np.zeros((t, LANES), np.float32)
    oh[np.arange(t), np.arange(t) // SEL_LEN] = 1.0

    e = np.zeros((3, LANES, NSA_REP * LANES), np.float32)
    for g in range(NSA_GROUPS):
        for r in range(NSA_REP):
            for c in range(3):
                col = r * LANES + g * HALF
                e[c, g * NSA_REP * 3 + r * 3 + c, col:col + HALF] = 1.0
    return (qf_da, qf_n, jnp.asarray(cpos), jnp.asarray(mt, BF), jnp.asarray(oh, BF), jnp.asarray(e, BF))


def _compress_weights(pe, w1, w2):
    hidden = w1.shape[1]
    w1r = w1.reshape(2, CMP_STRIDE, NSA_HEAD_DIM, hidden)
    w1g = jnp.zeros((2, NSA_GROUPS, CMP_STRIDE, NSA_GROUPS, NSA_HEAD_DIM, hidden), w1.dtype)
    for g in range(NSA_GROUPS):
        w1g = w1g.at[:, g, :, g].set(w1r)
    w1g = w1g.reshape(2, NSA_GROUPS, CMP_STRIDE * LANES, hidden).astype(BF)
    per = pe.reshape(2, CMP_STRIDE, 1, NSA_HEAD_DIM)
    peg = jnp.broadcast_to(per, (2, CMP_STRIDE, NSA_GROUPS, NSA_HEAD_DIM)).reshape(2, 1, CMP_STRIDE * LANES)
    w2g = jnp.zeros((NSA_GROUPS, hidden, NSA_GROUPS, NSA_HEAD_DIM), w2.dtype)
    for g in range(NSA_GROUPS):
        w2g = w2g.at[g, :, g].set(w2)
    w2g = w2g.reshape(NSA_GROUPS, hidden, LANES).astype(BF)
    return peg, w1g, w2g


def kernel(x, mem, ffn1_pre_g, ffn1_post_g, ffn1_w_gate, ffn1_w_up, ffn1_w_down, mix_pre_g, mix_post_g,
           w_mix_in, da_lambda_q1, da_lambda_k1, da_lambda_q2, da_lambda_k2, da_subln_g, cmp_k_pe,
           cmp_k_w1, cmp_k_w2, cmp_v_pe, cmp_v_w1, cmp_v_w2, w_mix_out, xa_pre_g, xa_post_g, mem_norm_g,
           xa_w_q, xa_w_k, xa_w_v, xa_w_o, ffn2_pre_g, ffn2_post_g, ffn2_w_gate, ffn2_w_up, ffn2_w_down):
    b, t, d = x.shape
    depth = ffn1_pre_g.shape[0]
    n = b * t
    tm = 512
    tq = 256
    ns = t // SEL_LEN
    assert t % tq == 0 and ns <= LANES and WIN % tq == 0 and d == DA_HEADS * DA_V_DIM + NSA_GROUPS * NSA_REP * NSA_HEAD_DIM
    qf_da, qf_n, cpos, mt, oh, e = _static_tables(t)
    row = lambda v: v.reshape(1, -1)

    for l in range(depth):
        x2 = _ffn(x.reshape(n, d), row(ffn1_pre_g[l]), row(ffn1_post_g[l]),
                  ffn1_w_gate[l].astype(BF), ffn1_w_up[l].astype(BF), ffn1_w_down[l].astype(BF), tm=tm)

        w = w_mix_in[l]
        dq = DA_HEADS * 2 * DA_QK_DIM
        dv = DA_HEADS * DA_V_DIM
        nqw = NSA_GROUPS * NSA_REP * NSA_HEAD_DIM
        nkv = NSA_GROUPS * NSA_HEAD_DIM
        o0 = 2 * dq + dv
        w_nq = w[:, o0:o0 + nqw].reshape(d, NSA_GROUPS, NSA_REP, NSA_HEAD_DIM).transpose(0, 2, 1, 3).reshape(d, nqw)
        o1 = o0 + nqw
        seg = lambda i: w[:, o1 + i * nkv:o1 + (i + 1) * nkv]
        w_g = w[:, o1 + 6 * nkv:]
        w_g = jnp.pad(w_g, ((0, 0), (0, LANES - w_g.shape[1])))
        w_big = jnp.concatenate([w[:, :o0], w_nq, seg(2), seg(4), seg(3), seg(5), seg(0), seg(1), w_g],
                                axis=1).astype(BF)
        (daq, dak, dav, nq, nks, nkw, nvs, nvw, nkc, nvc, gates) = _mixin(
            x2.reshape(b, t, d), row(mix_pre_g[l]), w_big, qf_da, qf_n, tm=tm)

        lam_init = 0.8 - 0.6 * float(np.exp(-0.3 * l))
        o_a = _da_attention(daq, dak, dav, row(da_lambda_q1[l]), row(da_lambda_k1[l]),
                            row(da_lambda_q2[l]), row(da_lambda_k2[l]), row(da_subln_g[l]),
                            lam_init=lam_init, tq=tq)

        pek, w1k, w2k = _compress_weights(cmp_k_pe[l], cmp_k_w1[l], cmp_k_w2[l])
        pev, w1v, w2v = _compress_weights(cmp_v_pe[l], cmp_v_w1[l], cmp_v_w2[l])
        ncp = t // CMP_STRIDE
        kc, vct = _compress(nkc.reshape(b, ncp, CMP_STRIDE * LANES), nvc.reshape(b, ncp, CMP_STRIDE * LANES),
                            pek, pev, w1k, w1v, w2k, w2v, cpos)
        o_cmp, sf = _cmp_topk(nq, kc, vct, mt, tq=tq, topk=min(SEL_TOPK, ns))
        o_sel = _nsa_flash(nq, nks, nvs, sf, oh, tq=tq, window=False)
        o_win = _nsa_flash(nq, nkw, nvw, tq=tq, window=True)

        wo = w_mix_out[l]
        wa = wo[:dv].astype(BF)
        wb = wo[dv:].reshape(NSA_GROUPS, NSA_REP, NSA_HEAD_DIM, d).transpose(1, 0, 2, 3).reshape(nqw, d).astype(BF)
        x3 = _mixout(x2, o_a.reshape(n, dv), o_cmp.reshape(n, nqw), o_sel.reshape(n, nqw),
                     o_win.reshape(n, nqw), gates.reshape(n, LANES), e, wa, wb, row(mix_post_g[l]), tm=tm)

        mk, mv = _memkv(mem, row(mem_norm_g[l]), xa_w_k[l].astype(BF), xa_w_v[l].astype(BF))
        x4 = _xa(x3.reshape(b, t, d), row(xa_pre_g[l]), row(xa_post_g[l]),
                 xa_w_q[l].astype(BF), xa_w_o[l].astype(BF), mk, mv, tm=tm)

        x = _ffn(x4.reshape(n, d), row(ffn2_pre_g[l]), row(ffn2_post_g[l]),
                 ffn2_w_gate[l].astype(BF), ffn2_w_up[l].astype(BF), ffn2_w_down[l].astype(BF),
                 tm=tm).reshape(b, t, d)
    return x
```

```python
import functools

import numpy as np
import jax
import jax.numpy as jnp
from jax import lax
from jax.experimental import pallas as pl
from jax.experimental.pallas import tpu as pltpu

BF = jnp.bfloat16
F32 = jnp.float32

EPS = 1e-6
NEG = -1e30
LANES = 128
HALF = LANES // 2
BF16_ROWS = 16

DA_HEADS = 4
DA_QK_DIM = 64
DA_V_DIM = 128
NSA_GROUPS = 2
NSA_REP = 4
NSA_HEAD_DIM = 64
CMP_LEN = 32
CMP_STRIDE = 16
SEL_LEN = 64
SEL_TOPK = 16
WIN = 512
FORCE_SCORE = 1e4
XA_HEADS = 4
SEL_MASK_BIG = 32768.0

VMEM_LIMIT = 56 * 1024 * 1024

_NT = (((1,), (1,)), ((), ()))


def _rms(x, g):
    ms = jnp.mean(x * x, axis=-1, keepdims=True)
    return x * lax.rsqrt(ms + EPS) * g


def _dot(a, b):
    return jnp.dot(a, b, preferred_element_type=F32)


def _dot_nt(a, b):
    return lax.dot_general(a, b, _NT, preferred_element_type=F32)


def _const_spec(shape):
    nd = len(shape)
    return pl.BlockSpec(shape, lambda *_: (0,) * nd, pipeline_mode=pl.Buffered(1))


def _alibi_slopes(n):
    return [2.0 ** (-8.0 * (i + 1) / n) for i in range(n)]


def _softmax_step(s, m_old, l_old, acc_old, vt, ones):
    m_new = jnp.maximum(m_old, jnp.max(s, axis=0, keepdims=True))
    alpha = jnp.exp(m_old - m_new)
    p = jnp.exp(s - m_new).astype(BF)
    l_new = alpha * l_old + _dot(ones, p)[0:1]
    acc_new = alpha * acc_old + _dot(vt, p)
    return m_new, l_new, acc_new


def _ffn_kernel(x_ref, pre_ref, post_ref, wg_ref, wu_ref, wd_ref, o_ref):
    x = x_ref[...]
    h = _rms(x, pre_ref[...]).astype(BF)
    g = _dot(h, wg_ref[...])
    u = _dot(h, wu_ref[...])
    a = (g * jax.nn.sigmoid(g) * u).astype(BF)
    y = _dot(a, wd_ref[...])
    o_ref[...] = x + 0.5 * _rms(y, post_ref[...])


def _ffn(x2, pre_g, post_g, wg, wu, wd, *, tm):
    n, d = x2.shape
    f = wg.shape[1]
    return pl.pallas_call(
        _ffn_kernel,
        out_shape=jax.ShapeDtypeStruct((n, d), F32),
        grid=(n // tm,),
        in_specs=[
            pl.BlockSpec((tm, d), lambda i: (i, 0)),
            _const_spec((1, d)), _const_spec((1, d)),
            _const_spec((d, f)), _const_spec((d, f)), _const_spec((f, d)),
        ],
        out_specs=pl.BlockSpec((tm, d), lambda i: (i, 0)),
        compiler_params=pltpu.CompilerParams(
            dimension_semantics=("arbitrary",), vmem_limit_bytes=VMEM_LIMIT),
        name="ffn",
    )(x2, pre_g, post_g, wg, wu, wd)


def _mixin_kernel(x_ref, g_ref, w_ref, qfda_ref, qfn_ref,
                  daq, dak, davt, nq, nks, nkw, nvst, nvwt, nkc, nvc, gat, *, tk_da, tk_nsa):
    tm = x_ref.shape[0]
    t0 = pl.program_id(1) * tm
    h = _rms(x_ref[...], g_ref[...]).astype(BF)
    z = _dot(h, w_ref[...])

    lane = lax.broadcasted_iota(jnp.int32, (tm, LANES), 1)
    row = lax.broadcasted_iota(jnp.int32, (tm, LANES), 0) + t0
    pa = (row >> 6).astype(F32)
    pb = (row & 63).astype(F32)
    low = lane < HALF
    pos_lo = jnp.where(lane == HALF, pa, jnp.where(lane == HALF + 1, pb, 0.0))
    pos_hi = jnp.where(lane == 0, pa, jnp.where(lane == 1, pb, 0.0))
    blk_onehot = jnp.where(lane == (row >> 6), 1.0, 0.0)

    def ch(i):
        return z[:, i * LANES:(i + 1) * LANES]

    def put(ref, j, val):
        ref[:, j * LANES:(j + 1) * LANES] = val.astype(ref.dtype)

    for c in range(4):
        q = ch(c) * 0.125
        put(daq, 2 * c, jnp.where(low, q, qfda_ref[:, (2 * c) * LANES:(2 * c + 1) * LANES]))
        put(daq, 2 * c + 1, jnp.where(low, qfda_ref[:, (2 * c + 1) * LANES:(2 * c + 2) * LANES], q))
        k = ch(4 + c)
        put(dak, 2 * c, jnp.where(low, k, pos_lo))
        put(dak, 2 * c + 1, jnp.where(low, pos_hi, k))
        vt = ch(8 + c).T
        for s in range(tm // tk_da):
            davt[s, c * LANES:(c + 1) * LANES, :] = vt[:, s * tk_da:(s + 1) * tk_da].astype(BF)
        q = ch(12 + c) * 0.125
        put(nq, 2 * c, jnp.where(low, q, qfn_ref[:, (2 * c) * LANES:(2 * c + 1) * LANES]))
        put(nq, 2 * c + 1, jnp.where(low, qfn_ref[:, (2 * c + 1) * LANES:(2 * c + 2) * LANES], q))
    k = ch(16)
    put(nks, 0, jnp.where(low, k, pos_lo))
    put(nks, 1, blk_onehot)
    put(nks, 2, jnp.where(low, pos_hi, k))
    put(nks, 3, blk_onehot)
    k = ch(17)
    put(nkw, 0, jnp.where(low, k, pos_lo))
    put(nkw, 1, jnp.where(low, pos_hi, k))
    for src, ref in ((18, nvst), (19, nvwt)):
        vt = ch(src).T
        for s in range(tm // tk_nsa):
            ref[s] = vt[:, s * tk_nsa:(s + 1) * tk_nsa].astype(BF)
    nkc[...] = ch(20)
    nvc[...] = ch(21)
    gat[...] = jax.nn.sigmoid(ch(22))


def _mixin(x, g, w, qf_da, qf_n, *, tm, tk_da, tk_nsa):
    b, t, d = x.shape
    wcols = w.shape[1]
    assert tm % tk_da == 0 and tm % tk_nsa == 0
    row = lambda wd, dt: (jax.ShapeDtypeStruct((b, t, wd), dt),
                          pl.BlockSpec((None, tm, wd), lambda i, j: (i, j, 0)))
    tr = lambda rows, tk: (jax.ShapeDtypeStruct((b, t // tk, rows, tk), BF),
                           pl.BlockSpec((None, tm // tk, rows, tk), lambda i, j: (i, j, 0, 0)))
    outs = [row(1024, BF), row(1024, BF), tr(512, tk_da), row(1024, BF), row(512, BF), row(256, BF),
            tr(LANES, tk_nsa), tr(LANES, tk_nsa), row(LANES, F32), row(LANES, F32), row(LANES, F32)]
    return pl.pallas_call(
        functools.partial(_mixin_kernel, tk_da=tk_da, tk_nsa=tk_nsa),
        out_shape=[o[0] for o in outs],
        grid=(b, t // tm),
        in_specs=[
            pl.BlockSpec((None, tm, d), lambda i, j: (i, j, 0)),
            _const_spec((1, d)), _const_spec((d, wcols)),
            _const_spec((1, 1024)), _const_spec((1, 1024)),
        ],
        out_specs=[o[1] for o in outs],
        compiler_params=pltpu.CompilerParams(
            dimension_semantics=("arbitrary", "arbitrary"), vmem_limit_bytes=VMEM_LIMIT),
        name="mixin",
    )(x, g, w, qf_da, qf_n)


def _da_kernel(q_ref, k_ref, vt_ref, lq1, lk1, lq2, lk2, sg_ref, o_ref, acc_ref, m_ref, l_ref,
               *, lam_init, tq):
    qi = pl.program_id(2)
    q = q_ref[...]
    m_ref[...] = jnp.full(m_ref.shape, NEG, F32)
    l_ref[...] = jnp.zeros(l_ref.shape, F32)
    acc_ref[...] = jnp.zeros(acc_ref.shape, F32)
    key = lax.broadcasted_iota(jnp.int32, (tq, tq), 0)
    qry = lax.broadcasted_iota(jnp.int32, (tq, tq), 1)
    ones = jnp.ones((BF16_ROWS, tq), BF)

    def step(j, masked):
        ks = pl.multiple_of(j * tq, tq)
        kt = k_ref[pl.ds(ks, tq), :]
        vt = vt_ref[j]
        for mp in range(2):
            s = _dot_nt(kt[:, mp * LANES:(mp + 1) * LANES], q[:, mp * LANES:(mp + 1) * LANES])
            if masked:
                s = jnp.where(key <= qry, s, NEG)
            m_ref[mp], l_ref[mp], acc_ref[mp] = _softmax_step(
                s, m_ref[mp], l_ref[mp], acc_ref[mp], vt, ones)

    def body(j, c):
        step(j, False)
        return c

    lax.fori_loop(0, qi, body, 0)
    step(qi, True)

    lam = (jnp.exp(jnp.sum(lq1[...] * lk1[...], axis=1, keepdims=True))
           - jnp.exp(jnp.sum(lq2[...] * lk2[...], axis=1, keepdims=True)) + lam_init)
    o = (acc_ref[0] / l_ref[0] - lam * (acc_ref[1] / l_ref[1])).T
    o_ref[...] = _rms(o, sg_ref[...]) * (1.0 - lam_init)


def _da_attention(daq, dak, davt, lq1, lk1, lq2, lk2, subln_g, *, lam_init, tq):
    b, t, _ = daq.shape
    nk = t // tq
    vec = _const_spec((1, DA_QK_DIM))
    return pl.pallas_call(
        functools.partial(_da_kernel, lam_init=lam_init, tq=tq),
        out_shape=jax.ShapeDtypeStruct((b, t, DA_HEADS * DA_V_DIM), F32),
        grid=(b, DA_HEADS, t // tq),
        in_specs=[
            pl.BlockSpec((None, tq, 2 * LANES), lambda i, h, j: (i, j, h)),
            pl.BlockSpec((None, t, 2 * LANES), lambda i, h, j: (i, 0, h)),
            pl.BlockSpec((None, nk, DA_V_DIM, tq), lambda i, h, j: (i, 0, h, 0)),
            vec, vec, vec, vec, _const_spec((1, DA_V_DIM)),
        ],
        out_specs=pl.BlockSpec((None, tq, DA_V_DIM), lambda i, h, j: (i, j, h)),
        scratch_shapes=[
            pltpu.VMEM((2, DA_V_DIM, tq), F32),
            pltpu.VMEM((2, 1, tq), F32),
            pltpu.VMEM((2, 1, tq), F32),
        ],
        compiler_params=pltpu.CompilerParams(
            dimension_semantics=("arbitrary", "arbitrary", "arbitrary"), vmem_limit_bytes=VMEM_LIMIT),
        name="diff_attn",
    )(daq, dak, davt, lq1, lk1, lq2, lk2, subln_g)


def _compress_kernel(xk_ref, xv_ref, pek_ref, pev_ref, w1k_ref, w1v_ref, w2k_ref, w2v_ref, cpos_ref,
                     kc_ref, vct_ref):
    ncp = xk_ref.shape[0]
    for x_ref, pe_ref, w1_ref, w2_ref, is_v in ((xk_ref, pek_ref, w1k_ref, w2k_ref, False),
                                                (xv_ref, pev_ref, w1v_ref, w2v_ref, True)):
        x = x_ref[...]
        xa = (x + pe_ref[0]).astype(BF)
        xb = (x + pe_ref[1]).astype(BF)
        for g in range(NSA_GROUPS):
            first = _dot(xa, w1_ref[0, g])
            second = _dot(xb, w1_ref[1, g])
            hid = first + pltpu.roll(second, ncp - 1, 0)
            act = (hid * jax.nn.sigmoid(hid)).astype(BF)
            out = _dot(act, w2_ref[g])
            if is_v:
                vct_ref[g] = out.T.astype(BF)
            else:
                kc_ref[g] = (out + cpos_ref[g]).astype(BF)


def _compress(xk, xv, pek, pev, w1k, w1v, w2k, w2v, cpos):
    b, ncp, width = xk.shape
    x_spec = pl.BlockSpec((None, ncp, width), lambda i: (i, 0, 0))
    return pl.pallas_call(
        _compress_kernel,
        out_shape=[jax.ShapeDtypeStruct((b, NSA_GROUPS, ncp, LANES), BF),
                   jax.ShapeDtypeStruct((b, NSA_GROUPS, LANES, ncp), BF)],
        grid=(b,),
        in_specs=[x_spec, x_spec,
                  _const_spec(pek.shape), _const_spec(pev.shape),
                  _const_spec(w1k.shape), _const_spec(w1v.shape),
                  _const_spec(w2k.shape), _const_spec(w2v.shape), _const_spec(cpos.shape)],
        out_specs=[pl.BlockSpec((None, NSA_GROUPS, ncp, LANES), lambda i: (i, 0, 0, 0)),
                   pl.BlockSpec((None, NSA_GROUPS, LANES, ncp), lambda i: (i, 0, 0, 0))],
        compiler_params=pltpu.CompilerParams(
            dimension_semantics=("arbitrary",), vmem_limit_bytes=VMEM_LIMIT),
        name="compress",
    )(xk, xv, pek, pev, w1k, w1v, w2k, w2v, cpos)


def _cmp_kernel(nq_ref, kc_ref, vct_ref, mt_ref, o_ref, sf_ref, ob_ref, *, tq, topk):
    ncp = kc_ref.shape[1]
    ns = mt_ref.shape[0]
    t0 = pl.program_id(1) * tq
    cidx = lax.broadcasted_iota(jnp.int32, (ncp, tq), 0)
    tpos = lax.broadcasted_iota(jnp.int32, (ncp, tq), 1) + t0
    valid = (cidx * CMP_STRIDE + (CMP_LEN - 1)) <= tpos
    lane = lax.broadcasted_iota(jnp.int32, (tq, LANES), 1)
    mt = mt_ref[...]

    blk = lax.broadcasted_iota(jnp.int32, (ns, tq), 0)
    cur = (lax.broadcasted_iota(jnp.int32, (ns, tq), 1) + t0) >> 6
    sub8 = lax.broadcasted_iota(jnp.int32, (8, tq), 0)

    for g in range(NSA_GROUPS):
        kc = kc_ref[g]
        vct = vct_ref[g]
        imp = jnp.zeros((ns, tq), F32)
        for r in range(NSA_REP):
            c = 2 * r + g
            q = nq_ref[:, c * LANES:(c + 1) * LANES]
            s = jnp.where(valid, _dot_nt(kc, q), NEG)
            m = jnp.max(s, axis=0, keepdims=True)
            e = jnp.exp(s - m)
            p = jnp.where(valid, e / jnp.sum(e, axis=0, keepdims=True), 0.0)
            p_hi = p.astype(BF)
            p_lo = (p - p_hi.astype(F32)).astype(BF)
            imp = imp + _dot(mt, p_hi) + _dot(mt, p_lo)
            o = _dot(vct, p_hi).T
            if g == 0:
                ob_ref[r] = o
            else:
                o_ref[:, r * LANES:(r + 1) * LANES] = jnp.where(lane < HALF, ob_ref[r], o)

        score = jnp.where(blk == 0, FORCE_SCORE,
                          jnp.where(blk == cur, FORCE_SCORE,
                                    jnp.where(blk == cur - 1, FORCE_SCORE,
                                              jnp.where(blk <= cur, imp, -1.0))))
        rows = [jnp.broadcast_to(score[i:i + 1, :], (8, tq)) for i in range(ns)]
        feats = []
        for kt in range(ns // 8):
            sc = score[8 * kt:8 * kt + 8, :]
            cnt = jnp.zeros((8, tq), F32)
            for i in range(ns):
                if i < 8 * kt:
                    beat = jnp.where(rows[i] >= sc, 1.0, 0.0)
                elif i >= 8 * kt + 8:
                    beat = jnp.where(rows[i] > sc, 1.0, 0.0)
                else:
                    beat = jnp.where(sub8 + 8 * kt > i,
                                     jnp.where(rows[i] >= sc, 1.0, 0.0),
                                     jnp.where(rows[i] > sc, 1.0, 0.0))
                cnt = cnt + beat
            feats.append(jnp.where(cnt < topk, 0.0, -SEL_MASK_BIG))
        feats.append(jnp.zeros((LANES - ns, tq), F32))
        sf_ref[g] = jnp.concatenate(feats, axis=0).T.astype(BF)


def _cmp_topk(nq, kc, vct, mt, *, tq, topk):
    b, t, _ = nq.shape
    ncp = kc.shape[2]
    return pl.pallas_call(
        functools.partial(_cmp_kernel, tq=tq, topk=topk),
        out_shape=[jax.ShapeDtypeStruct((b, t, NSA_REP * LANES), F32),
                   jax.ShapeDtypeStruct((b, NSA_GROUPS, t, LANES), BF)],
        grid=(b, t // tq),
        in_specs=[
            pl.BlockSpec((None, tq, 2 * NSA_REP * LANES), lambda i, j: (i, j, 0)),
            pl.BlockSpec((None, NSA_GROUPS, ncp, LANES), lambda i, j: (i, 0, 0, 0)),
            pl.BlockSpec((None, NSA_GROUPS, LANES, ncp), lambda i, j: (i, 0, 0, 0)),
            _const_spec(mt.shape),
        ],
        out_specs=[pl.BlockSpec((None, tq, NSA_REP * LANES), lambda i, j: (i, j, 0)),
                   pl.BlockSpec((None, NSA_GROUPS, tq, LANES), lambda i, j: (i, 0, j, 0))],
        scratch_shapes=[pltpu.VMEM((NSA_REP, tq, LANES), F32)],
        compiler_params=pltpu.CompilerParams(
            dimension_semantics=("arbitrary", "arbitrary"), vmem_limit_bytes=VMEM_LIMIT),
        name="cmp_topk",
    )(nq, kc, vct, mt)


def _nsa_flash_kernel(*refs, tq, window, selected):
    if selected:
        nq_ref, sf_ref, k_ref, vt_ref, o_ref, qs_ref, acc_ref, m_ref, l_ref, ob_ref = refs
    else:
        nq_ref, k_ref, vt_ref, o_ref, qs_ref, acc_ref, m_ref, l_ref, ob_ref = refs
    qi = pl.program_id(1)
    kd = qs_ref.shape[2]
    cols = NSA_REP * tq
    key = lax.broadcasted_iota(jnp.int32, (tq, cols), 0)
    qry = lax.broadcasted_iota(jnp.int32, (tq, cols), 1) & (tq - 1)
    ones = jnp.ones((BF16_ROWS, tq), BF)

    for g in range(NSA_GROUPS):
        for r in range(NSA_REP):
            c = 2 * r + g
            qs_ref[g, r * tq:(r + 1) * tq, 0:LANES] = nq_ref[:, c * LANES:(c + 1) * LANES]
            if selected:
                qs_ref[g, r * tq:(r + 1) * tq, LANES:2 * LANES] = sf_ref[g]

    m_ref[...] = jnp.full(m_ref.shape, NEG, F32)
    l_ref[...] = jnp.zeros(l_ref.shape, F32)
    acc_ref[...] = jnp.zeros(acc_ref.shape, F32)

    def step(j, mode):
        ks = pl.multiple_of(j * tq, tq)
        for g in range(NSA_GROUPS):
            s = _dot_nt(k_ref[pl.ds(ks, tq), g * kd:(g + 1) * kd], qs_ref[g])
            if mode == "diag":
                s = jnp.where(key <= qry, s, NEG)
            elif mode == "edge":
                s = jnp.where(key > qry, s, NEG)
            m_ref[g], l_ref[g], acc_ref[g] = _softmax_step(
                s, m_ref[g], l_ref[g], acc_ref[g],
                vt_ref[j, g * NSA_HEAD_DIM:(g + 1) * NSA_HEAD_DIM, :], ones)

    def body(j, carry):
        step(j, "full")
        return carry

    if window:
        nback = WIN // tq

        @pl.when(qi >= nback)
        def _():
            step(qi - nback, "edge")

        lax.fori_loop(jnp.maximum(qi - nback + 1, 0), qi, body, 0)
    else:
        lax.fori_loop(0, qi, body, 0)
    step(qi, "diag")

    for g in range(NSA_GROUPS):
        o = acc_ref[g] / l_ref[g]
        for r in range(NSA_REP):
            ob_ref[r, g * NSA_HEAD_DIM:(g + 1) * NSA_HEAD_DIM, :] = o[:, r * tq:(r + 1) * tq]

    for r in range(NSA_REP):
        o_ref[:, r * LANES:(r + 1) * LANES] = ob_ref[r].T


def _nsa_flash(nq, k, vt, sf=None, *, tq, window):
    b, t, _ = nq.shape
    selected = sf is not None
    kd = 2 * LANES if selected else LANES
    nk = t // tq
    assert vt.shape == (b, nk, LANES, tq) and k.shape == (b, t, NSA_GROUPS * kd)
    in_specs = [pl.BlockSpec((None, tq, 2 * NSA_REP * LANES), lambda i, j: (i, j, 0))]
    args = [nq]
    if selected:
        in_specs.append(pl.BlockSpec((None, NSA_GROUPS, tq, LANES), lambda i, j: (i, 0, j, 0)))
        args.append(sf)
    in_specs.append(pl.BlockSpec((None, t, NSA_GROUPS * kd), lambda i, j: (i, 0, 0)))
    args.append(k)
    in_specs.append(pl.BlockSpec((None, nk, LANES, tq), lambda i, j: (i, 0, 0, 0)))
    args.append(vt)
    cols = NSA_REP * tq
    return pl.pallas_call(
        functools.partial(_nsa_flash_kernel, tq=tq, window=window, selected=selected),
        out_shape=jax.ShapeDtypeStruct((b, t, NSA_REP * LANES), F32),
        grid=(b, t // tq),
        in_specs=in_specs,
        out_specs=pl.BlockSpec((None, tq, NSA_REP * LANES), lambda i, j: (i, j, 0)),
        scratch_shapes=[
            pltpu.VMEM((NSA_GROUPS, cols, kd), BF),
            pltpu.VMEM((NSA_GROUPS, NSA_HEAD_DIM, cols), F32),
            pltpu.VMEM((NSA_GROUPS, 1, cols), F32),
            pltpu.VMEM((NSA_GROUPS, 1, cols), F32),
            pltpu.VMEM((NSA_REP, LANES, tq), F32),
        ],
        compiler_params=pltpu.CompilerParams(
            dimension_semantics=("arbitrary", "arbitrary"), vmem_limit_bytes=VMEM_LIMIT),
        name="nsa_window" if window else "nsa_selected",
    )(*args)


def _mixout_kernel(x_ref, oa_ref, oc_ref, os_ref, ow_ref, gat_ref, e_ref, wa_ref, wb_ref, post_ref, o_ref):
    gt = gat_ref[...]
    g_hi = gt.astype(BF)
    g_lo = (gt - g_hi.astype(F32)).astype(BF)
    ob = None
    for c, ref in enumerate((oc_ref, os_ref, ow_ref)):
        gate = _dot(g_hi, e_ref[c]) + _dot(g_lo, e_ref[c])
        term = gate * ref[...]
        ob = term if ob is None else ob + term
    y = _dot(oa_ref[...].astype(BF), wa_ref[...]) + _dot(ob.astype(BF), wb_ref[...])
    o_ref[...] = x_ref[...] + _rms(y, post_ref[...])


def _mixout(x2, oa, oc, osel, ow, gates, e, wa, wb, post_g, *, tm):
    n, d = x2.shape
    row = lambda w: pl.BlockSpec((tm, w), lambda i: (i, 0))
    return pl.pallas_call(
        _mixout_kernel,
        out_shape=jax.ShapeDtypeStruct((n, d), F32),
        grid=(n // tm,),
        in_specs=[row(d), row(512), row(512), row(512), row(512), row(LANES),
                  _const_spec(e.shape), _const_spec(wa.shape), _const_spec(wb.shape), _const_spec((1, d))],
        out_specs=row(d),
        compiler_params=pltpu.CompilerParams(
            dimension_semantics=("arbitrary",), vmem_limit_bytes=VMEM_LIMIT),
        name="mixout",
    )(x2, oa, oc, osel, ow, gates, e, wa, wb, post_g)


def _memkv_kernel(m_ref, g_ref, wk_ref, wv_ref, k_ref, v_ref):
    m = _rms(m_ref[...], g_ref[...]).astype(BF)
    k_ref[...] = _dot(m, wk_ref[...]).astype(BF)
    v_ref[...] = _dot(m, wv_ref[...]).astype(BF)


def _memkv(mem, g, wk, wv):
    b, nm, d = mem.shape
    spec = pl.BlockSpec((None, nm, d), lambda i: (i, 0, 0))
    return pl.pallas_call(
        _memkv_kernel,
        out_shape=[jax.ShapeDtypeStruct((b, nm, d), BF)] * 2,
        grid=(b,),
        in_specs=[spec, _const_spec((1, d)), _const_spec(wk.shape), _const_spec(wv.shape)],
        out_specs=[spec, spec],
        compiler_params=pltpu.CompilerParams(
            dimension_semantics=("arbitrary",), vmem_limit_bytes=VMEM_LIMIT),
        name="mem_kv",
    )(mem, g, wk, wv)


def _xa_kernel(x_ref, pre_ref, post_ref, wq_ref, wo_ref, k_ref, v_ref, o_ref, *, scale):
    x = x_ref[...]
    h = _rms(x, pre_ref[...]).astype(BF)
    q = (_dot(h, wq_ref[...]) * scale).astype(BF)
    hd = q.shape[1] // XA_HEADS
    outs = []
    for i in range(XA_HEADS):
        sl = slice(i * hd, (i + 1) * hd)
        s = _dot_nt(q[:, sl], k_ref[:, sl])
        m = jnp.max(s, axis=1, keepdims=True)
        e = jnp.exp(s - m)
        p = e / jnp.sum(e, axis=1, keepdims=True)
        outs.append(_dot(p.astype(BF), v_ref[:, sl]).astype(BF))
    y = _dot(jnp.concatenate(outs, axis=1), wo_ref[...])
    o_ref[...] = x + _rms(y, post_ref[...])


def _xa(x, pre_g, post_g, wq, wo, k, v, *, tm):
    b, t, d = x.shape
    nm = k.shape[1]
    hd = wq.shape[1] // XA_HEADS
    row = pl.BlockSpec((None, tm, d), lambda i, j: (i, j, 0))
    kv = pl.BlockSpec((None, nm, k.shape[2]), lambda i, j: (i, 0, 0))
    return pl.pallas_call(
        functools.partial(_xa_kernel, scale=float(hd) ** -0.5),
        out_shape=jax.ShapeDtypeStruct((b, t, d), F32),
        grid=(b, t // tm),
        in_specs=[row, _const_spec((1, d)), _const_spec((1, d)),
                  _const_spec(wq.shape), _const_spec(wo.shape), kv, kv],
        out_specs=row,
        compiler_params=pltpu.CompilerParams(
            dimension_semantics=("arbitrary", "arbitrary"), vmem_limit_bytes=VMEM_LIMIT),
        name="mem_xattn",
    )(x, pre_g, post_g, wq, wo, k, v)


def _query_feature_rows(slopes_by_chunk):
    out = np.zeros((1, len(slopes_by_chunk) * LANES), np.float32)
    for j, s in enumerate(slopes_by_chunk):
        base = j * LANES + (HALF if j % 2 == 0 else 0)
        out[0, base] = s * 64.0
        out[0, base + 1] = s
    return jnp.asarray(out)


def _static_tables(t):
    ncp = t // CMP_STRIDE
    ns = t // SEL_LEN
    da = _alibi_slopes(DA_HEADS)
    qf_da = _query_feature_rows([da[j // 2] for j in range(2 * DA_HEADS)])
    nsa = _alibi_slopes(NSA_GROUPS * NSA_REP)
    qf_n = _query_feature_rows([nsa[(j % 2) * NSA_REP + j // 2] for j in range(2 * NSA_REP)])

    c_end = np.arange(ncp) * CMP_STRIDE + CMP_LEN - 1
    cpos = np.zeros((NSA_GROUPS, ncp, LANES), np.float32)
    cpos[0, :, HALF] = c_end // 64
    cpos[0, :, HALF + 1] = c_end % 64
    cpos[1, :, 0] = c_end // 64
    cpos[1, :, 1] = c_end % 64

    c_start = np.arange(ncp) * CMP_STRIDE
    s_start = np.arange(ns) * SEL_LEN
    overlap = np.clip(np.minimum(c_start[:, None] + CMP_LEN, s_start[None, :] + SEL_LEN)
                      - np.maximum(c_start[:, None], s_start[None, :]), 0, None)
    mt = (overlap.astype(np.float32) / CMP_LEN).T
    mt[:, ncp - 1] = 0.0

    e = np.zeros((3, LANES, NSA_REP * LANES), np.float32)
    for g in range(NSA_GROUPS):
        for r in range(NSA_REP):
            for c in range(3):
                col = r * LANES + g * HALF
                e[c, g * NSA_REP * 3 + r * 3 + c, col:col + HALF] = 1.0
    return qf_da, qf_n, jnp.asarray(cpos), jnp.asarray(mt, BF), jnp.asarray(e, BF)


def _compress_weights(pe, w1, w2):
    hidden = w1.shape[1]
    w1r = w1.reshape(2, CMP_STRIDE, NSA_HEAD_DIM, hidden)
    w1g = jnp.zeros((2, NSA_GROUPS, CMP_STRIDE, NSA_GROUPS, NSA_HEAD_DIM, hidden), w1.dtype)
    for g in range(NSA_GROUPS):
        w1g = w1g.at[:, g, :, g].set(w1r)
    w1g = w1g.reshape(2, NSA_GROUPS, CMP_STRIDE * LANES, hidden).astype(BF)
    per = pe.reshape(2, CMP_STRIDE, 1, NSA_HEAD_DIM)
    peg = jnp.broadcast_to(per, (2, CMP_STRIDE, NSA_GROUPS, NSA_HEAD_DIM)).reshape(2, 1, CMP_STRIDE * LANES)
    w2g = jnp.zeros((NSA_GROUPS, hidden, NSA_GROUPS, NSA_HEAD_DIM), w2.dtype)
    for g in range(NSA_GROUPS):
        w2g = w2g.at[g, :, g].set(w2)
    w2g = w2g.reshape(NSA_GROUPS, hidden, LANES).astype(BF)
    return peg, w1g, w2g


def kernel(x, mem, ffn1_pre_g, ffn1_post_g, ffn1_w_gate, ffn1_w_up, ffn1_w_down, mix_pre_g, mix_post_g,
           w_mix_in, da_lambda_q1, da_lambda_k1, da_lambda_q2, da_lambda_k2, da_subln_g, cmp_k_pe,
           cmp_k_w1, cmp_k_w2, cmp_v_pe, cmp_v_w1, cmp_v_w2, w_mix_out, xa_pre_g, xa_post_g, mem_norm_g,
           xa_w_q, xa_w_k, xa_w_v, xa_w_o, ffn2_pre_g, ffn2_post_g, ffn2_w_gate, ffn2_w_up, ffn2_w_down):
    b, t, d = x.shape
    depth = ffn1_pre_g.shape[0]
    n = b * t
    tm = 512
    tq_da = 512
    tq = 256
    ns = t // SEL_LEN
    assert t % tm == 0 and ns <= LANES and WIN % tq == 0
    assert d == DA_HEADS * DA_V_DIM + NSA_GROUPS * NSA_REP * NSA_HEAD_DIM
    qf_da, qf_n, cpos, mt, e = _static_tables(t)
    row = lambda v: v.reshape(1, -1)

    for l in range(depth):
        x2 = _ffn(x.reshape(n, d), row(ffn1_pre_g[l]), row(ffn1_post_g[l]),
                  ffn1_w_gate[l].astype(BF), ffn1_w_up[l].astype(BF), ffn1_w_down[l].astype(BF), tm=tm)

        w = w_mix_in[l]
        dq = DA_HEADS * 2 * DA_QK_DIM
        dv = DA_HEADS * DA_V_DIM
        nqw = NSA_GROUPS * NSA_REP * NSA_HEAD_DIM
        nkv = NSA_GROUPS * NSA_HEAD_DIM
        o0 = 2 * dq + dv
        w_nq = w[:, o0:o0 + nqw].reshape(d, NSA_GROUPS, NSA_REP, NSA_HEAD_DIM).transpose(0, 2, 1, 3).reshape(d, nqw)
        o1 = o0 + nqw
        seg = lambda i: w[:, o1 + i * nkv:o1 + (i + 1) * nkv]
        w_g = w[:, o1 + 6 * nkv:]
        w_g = jnp.pad(w_g, ((0, 0), (0, LANES - w_g.shape[1])))
        w_big = jnp.concatenate([w[:, :o0], w_nq, seg(2), seg(4), seg(3), seg(5), seg(0), seg(1), w_g],
                                axis=1).astype(BF)
        (daq, dak, davt, nq, nks, nkw, nvst, nvwt, nkc, nvc, gates) = _mixin(
            x2.reshape(b, t, d), row(mix_pre_g[l]), w_big, qf_da, qf_n, tm=tm, tk_da=tq_da, tk_nsa=tq)

        lam_init = 0.8 - 0.6 * float(np.exp(-0.3 * l))
        o_a = _da_attention(daq, dak, davt, row(da_lambda_q1[l]), row(da_lambda_k1[l]),
                            row(da_lambda_q2[l]), row(da_lambda_k2[l]), row(da_subln_g[l]),
                            lam_init=lam_init, tq=tq_da)

        pek, w1k, w2k = _compress_weights(cmp_k_pe[l], cmp_k_w1[l], cmp_k_w2[l])
        pev, w1v, w2v = _compress_weights(cmp_v_pe[l], cmp_v_w1[l], cmp_v_w2[l])
        ncp = t // CMP_STRIDE
        kc, vct = _compress(nkc.reshape(b, ncp, CMP_STRIDE * LANES), nvc.reshape(b, ncp, CMP_STRIDE * LANES),
                            pek, pev, w1k, w1v, w2k, w2v, cpos)
        o_cmp, sf = _cmp_topk(nq, kc, vct, mt, tq=tq, topk=min(SEL_TOPK, ns))
        o_sel = _nsa_flash(nq, nks, nvst, sf, tq=tq, window=False)
        o_win = _nsa_flash(nq, nkw, nvwt, tq=tq, window=True)

        wo = w_mix_out[l]
        wa = wo[:dv].astype(BF)
        wb = wo[dv:].reshape(NSA_GROUPS, NSA_REP, NSA_HEAD_DIM, d).transpose(1, 0, 2, 3).reshape(nqw, d).astype(BF)
        x3 = _mixout(x2, o_a.reshape(n, dv), o_cmp.reshape(n, nqw), o_sel.reshape(n, nqw),
                     o_win.reshape(n, nqw), gates.reshape(n, LANES), e, wa, wb, row(mix_post_g[l]), tm=tm)

        mk, mv = _memkv(mem, row(mem_norm_g[l]), xa_w_k[l].astype(BF), xa_w_v[l].astype(BF))
        x4 = _xa(x3.reshape(b, t, d), row(xa_pre_g[l]), row(xa_post_g[l]),
                 xa_w_q[l].astype(BF), xa_w_o[l].astype(BF), mk, mv, tm=tm)

        x = _ffn(x4.reshape(n, d), row(ffn2_pre_g[l]), row(ffn2_post_g[l]),
                 ffn2_w_gate[l].astype(BF), ffn2_w_up[l].astype(BF), ffn2_w_down[l].astype(BF),
                 tm=tm).reshape(b, t, d)
    return x
```

```python
import functools

import numpy as np
import jax
import jax.numpy as jnp
from jax import lax
from jax.experimental import pallas as pl
from jax.experimental.pallas import tpu as pltpu

BF = jnp.bfloat16
F32 = jnp.float32

EPS = 1e-6
NEG = -1e30
LANES = 128
HALF = LANES // 2
BF16_ROWS = 16

DA_HEADS = 4
DA_QK_DIM = 64
DA_V_DIM = 128
NSA_GROUPS = 2
NSA_REP = 4
NSA_HEAD_DIM = 64
CMP_LEN = 32
CMP_STRIDE = 16
SEL_LEN = 64
SEL_TOPK = 16
WIN = 512
FORCE_SCORE = 1e4
XA_HEADS = 4
SEL_MASK_BIG = 32768.0

VMEM_LIMIT = 56 * 1024 * 1024

_NT = (((1,), (1,)), ((), ()))


def _rms(x, g):
    ms = jnp.mean(x * x, axis=-1, keepdims=True)
    return x * lax.rsqrt(ms + EPS) * g


def _dot(a, b):
    return jnp.dot(a, b, preferred_element_type=F32)


def _dot_nt(a, b):
    return lax.dot_general(a, b, _NT, preferred_element_type=F32)


def _const_spec(shape):
    nd = len(shape)
    return pl.BlockSpec(shape, lambda *_: (0,) * nd, pipeline_mode=pl.Buffered(1))


def _alibi_slopes(n):
    return [2.0 ** (-8.0 * (i + 1) / n) for i in range(n)]


def _softmax_step(s, m_old, l_old, acc_old, vt, ones):
    m_new = jnp.maximum(m_old, jnp.max(s, axis=0, keepdims=True))
    alpha = jnp.exp(m_old - m_new)
    p = jnp.exp(s - m_new).astype(BF)
    l_new = alpha * l_old + _dot(ones, p)[0:1]
    acc_new = alpha * acc_old + _dot(vt, p)
    return m_new, l_new, acc_new


def _ffn_kernel(x_ref, pre_ref, post_ref, wg_ref, wu_ref, wd_ref, o_ref):
    x = x_ref[...]
    h = _rms(x, pre_ref[...]).astype(BF)
    g = _dot(h, wg_ref[...])
    u = _dot(h, wu_ref[...])
    a = (g * jax.nn.sigmoid(g) * u).astype(BF)
    y = _dot(a, wd_ref[...])
    o_ref[...] = x + 0.5 * _rms(y, post_ref[...])


def _ffn(x2, pre_g, post_g, wg, wu, wd, *, tm):
    n, d = x2.shape
    f = wg.shape[1]
    return pl.pallas_call(
        _ffn_kernel,
        out_shape=jax.ShapeDtypeStruct((n, d), F32),
        grid=(n // tm,),
        in_specs=[
            pl.BlockSpec((tm, d), lambda i: (i, 0)),
            _const_spec((1, d)), _const_spec((1, d)),
            _const_spec((d, f)), _const_spec((d, f)), _const_spec((f, d)),
        ],
        out_specs=pl.BlockSpec((tm, d), lambda i: (i, 0)),
        compiler_params=pltpu.CompilerParams(
            dimension_semantics=("arbitrary",), vmem_limit_bytes=VMEM_LIMIT),
        name="ffn",
    )(x2, pre_g, post_g, wg, wu, wd)


def _mixin_kernel(x_ref, g_ref, w_ref, qfda_ref, qfn_ref,
                  daq, dak, davt, nq, nks, nkw, nvst, nvwt, nkc, nvc, gat, *, tk_da, tk_nsa):
    tm = x_ref.shape[0]
    t0 = pl.program_id(1) * tm
    h = _rms(x_ref[...], g_ref[...]).astype(BF)
    z = _dot(h, w_ref[...])

    lane = lax.broadcasted_iota(jnp.int32, (tm, LANES), 1)
    row = lax.broadcasted_iota(jnp.int32, (tm, LANES), 0) + t0
    pa = (row >> 6).astype(F32)
    pb = (row & 63).astype(F32)
    low = lane < HALF
    pos_lo = jnp.where(lane == HALF, pa, jnp.where(lane == HALF + 1, pb, 0.0))
    pos_hi = jnp.where(lane == 0, pa, jnp.where(lane == 1, pb, 0.0))
    blk_onehot = jnp.where(lane == (row >> 6), 1.0, 0.0)

    def ch(i):
        return z[:, i * LANES:(i + 1) * LANES]

    def put(ref, j, val):
        ref[:, j * LANES:(j + 1) * LANES] = val.astype(ref.dtype)

    for c in range(4):
        q = ch(c) * 0.125
        put(daq, 2 * c, jnp.where(low, q, qfda_ref[:, (2 * c) * LANES:(2 * c + 1) * LANES]))
        put(daq, 2 * c + 1, jnp.where(low, qfda_ref[:, (2 * c + 1) * LANES:(2 * c + 2) * LANES], q))
        k = ch(4 + c)
        put(dak, 2 * c, jnp.where(low, k, pos_lo))
        put(dak, 2 * c + 1, jnp.where(low, pos_hi, k))
        vt = ch(8 + c).T
        for s in range(tm // tk_da):
            davt[s, c * LANES:(c + 1) * LANES, :] = vt[:, s * tk_da:(s + 1) * tk_da].astype(BF)
        q = ch(12 + c) * 0.125
        put(nq, 2 * c, jnp.where(low, q, qfn_ref[:, (2 * c) * LANES:(2 * c + 1) * LANES]))
        put(nq, 2 * c + 1, jnp.where(low, qfn_ref[:, (2 * c + 1) * LANES:(2 * c + 2) * LANES], q))
    k = ch(16)
    put(nks, 0, jnp.where(low, k, pos_lo))
    put(nks, 1, blk_onehot)
    put(nks, 2, jnp.where(low, pos_hi, k))
    put(nks, 3, blk_onehot)
    k = ch(17)
    put(nkw, 0, jnp.where(low, k, pos_lo))
    put(nkw, 1, jnp.where(low, pos_hi, k))
    for src, ref in ((18, nvst), (19, nvwt)):
        vt = ch(src).T
        for s in range(tm // tk_nsa):
            ref[s] = vt[:, s * tk_nsa:(s + 1) * tk_nsa].astype(BF)
    nkc[...] = ch(20)
    nvc[...] = ch(21)
    gat[...] = jax.nn.sigmoid(ch(22))


def _mixin(x, g, w, qf_da, qf_n, *, tm, tk_da, tk_nsa):
    b, t, d = x.shape
    wcols = w.shape[1]
    assert tm % tk_da == 0 and tm % tk_nsa == 0
    row = lambda wd, dt: (jax.ShapeDtypeStruct((b, t, wd), dt),
                          pl.BlockSpec((None, tm, wd), lambda i, j: (i, j, 0)))
    tr = lambda rows, tk: (jax.ShapeDtypeStruct((b, t // tk, rows, tk), BF),
                           pl.BlockSpec((None, tm // tk, rows, tk), lambda i, j: (i, j, 0, 0)))
    outs = [row(1024, BF), row(1024, BF), tr(512, tk_da), row(1024, BF), row(512, BF), row(256, BF),
            tr(LANES, tk_nsa), tr(LANES, tk_nsa), row(LANES, F32), row(LANES, F32), row(LANES, F32)]
    return pl.pallas_call(
        functools.partial(_mixin_kernel, tk_da=tk_da, tk_nsa=tk_nsa),
        out_shape=[o[0] for o in outs],
        grid=(b, t // tm),
        in_specs=[
            pl.BlockSpec((None, tm, d), lambda i, j: (i, j, 0)),
            _const_spec((1, d)), _const_spec((d, wcols)),
            _const_spec((1, 1024)), _const_spec((1, 1024)),
        ],
        out_specs=[o[1] for o in outs],
        compiler_params=pltpu.CompilerParams(
            dimension_semantics=("arbitrary", "arbitrary"), vmem_limit_bytes=VMEM_LIMIT),
        name="mixin",
    )(x, g, w, qf_da, qf_n)


def _da_kernel(q_ref, k_ref, vt_ref, lq1, lk1, lq2, lk2, sg_ref, o_ref, s_ref, acc_ref, m_ref, l_ref,
               *, lam_init, tq, strip):
    qi = pl.program_id(2)
    m_ref[...] = jnp.full(m_ref.shape, NEG, F32)
    l_ref[...] = jnp.zeros(l_ref.shape, F32)
    acc_ref[...] = jnp.zeros(acc_ref.shape, F32)
    key = lax.broadcasted_iota(jnp.int32, (tq, strip), 0)
    qry = lax.broadcasted_iota(jnp.int32, (tq, strip), 1)
    ones = jnp.ones((BF16_ROWS, tq), BF)

    def scores(j, buf):
        ks = pl.multiple_of(j * tq, tq)
        for mp in range(2):
            s_ref[buf, mp] = _dot_nt(k_ref[pl.ds(ks, tq), mp * LANES:(mp + 1) * LANES],
                                     q_ref[:, mp * LANES:(mp + 1) * LANES])

    def consume(j, buf, masked):
        vt = vt_ref[j]
        for mp in range(2):
            for c in range(tq // strip):
                sl = slice(c * strip, (c + 1) * strip)
                s = s_ref[buf, mp, :, sl]
                if masked:
                    s = jnp.where(key <= qry + c * strip, s, NEG)
                m_ref[mp, :, sl], l_ref[mp, :, sl], acc_ref[mp, :, sl] = _softmax_step(
                    s, m_ref[mp, :, sl], l_ref[mp, :, sl], acc_ref[mp, :, sl], vt, ones)

    scores(0, 0)

    def pair(p, carry):
        scores(2 * p + 1, 1)
        consume(2 * p, 0, False)
        scores(2 * p + 2, 0)
        consume(2 * p + 1, 1, False)
        return carry

    lax.fori_loop(0, qi // 2, pair, 0)

    @pl.when(qi % 2 == 1)
    def _():
        scores(qi, 1)
        consume(qi - 1, 0, False)
        consume(qi, 1, True)

    @pl.when(qi % 2 == 0)
    def _():
        consume(qi, 0, True)

    lam = (jnp.exp(jnp.sum(lq1[...] * lk1[...], axis=1, keepdims=True))
           - jnp.exp(jnp.sum(lq2[...] * lk2[...], axis=1, keepdims=True)) + lam_init)
    o = (acc_ref[0] / l_ref[0] - lam * (acc_ref[1] / l_ref[1])).T
    o_ref[...] = _rms(o, sg_ref[...]) * (1.0 - lam_init)


def _da_attention(daq, dak, davt, lq1, lk1, lq2, lk2, subln_g, *, lam_init, tq):
    b, t, _ = daq.shape
    nk = t // tq
    vec = _const_spec((1, DA_QK_DIM))
    return pl.pallas_call(
        functools.partial(_da_kernel, lam_init=lam_init, tq=tq, strip=2 * LANES),
        out_shape=jax.ShapeDtypeStruct((b, t, DA_HEADS * DA_V_DIM), F32),
        grid=(b, DA_HEADS, t // tq),
        in_specs=[
            pl.BlockSpec((None, tq, 2 * LANES), lambda i, h, j: (i, j, h)),
            pl.BlockSpec((None, t, 2 * LANES), lambda i, h, j: (i, 0, h)),
            pl.BlockSpec((None, nk, DA_V_DIM, tq), lambda i, h, j: (i, 0, h, 0)),
            vec, vec, vec, vec, _const_spec((1, DA_V_DIM)),
        ],
        out_specs=pl.BlockSpec((None, tq, DA_V_DIM), lambda i, h, j: (i, j, h)),
        scratch_shapes=[
            pltpu.VMEM((2, 2, tq, tq), F32),
            pltpu.VMEM((2, DA_V_DIM, tq), F32),
            pltpu.VMEM((2, 1, tq), F32),
            pltpu.VMEM((2, 1, tq), F32),
        ],
        compiler_params=pltpu.CompilerParams(
            dimension_semantics=("arbitrary", "arbitrary", "arbitrary"), vmem_limit_bytes=VMEM_LIMIT),
        name="diff_attn",
    )(daq, dak, davt, lq1, lk1, lq2, lk2, subln_g)


def _compress_kernel(xk_ref, xv_ref, pek_ref, pev_ref, w1k_ref, w1v_ref, w2k_ref, w2v_ref, cpos_ref,
                     kc_ref, vct_ref):
    ncp = xk_ref.shape[0]
    for x_ref, pe_ref, w1_ref, w2_ref, is_v in ((xk_ref, pek_ref, w1k_ref, w2k_ref, False),
                                                (xv_ref, pev_ref, w1v_ref, w2v_ref, True)):
        x = x_ref[...]
        xa = (x + pe_ref[0]).astype(BF)
        xb = (x + pe_ref[1]).astype(BF)
        for g in range(NSA_GROUPS):
            first = _dot(xa, w1_ref[0, g])
            second = _dot(xb, w1_ref[1, g])
            hid = first + pltpu.roll(second, ncp - 1, 0)
            act = (hid * jax.nn.sigmoid(hid)).astype(BF)
            out = _dot(act, w2_ref[g])
            if is_v:
                vct_ref[g] = out.T.astype(BF)
            else:
                kc_ref[g] = (out + cpos_ref[g]).astype(BF)


def _compress(xk, xv, pek, pev, w1k, w1v, w2k, w2v, cpos):
    b, ncp, width = xk.shape
    x_spec = pl.BlockSpec((None, ncp, width), lambda i: (i, 0, 0))
    return pl.pallas_call(
        _compress_kernel,
        out_shape=[jax.ShapeDtypeStruct((b, NSA_GROUPS, ncp, LANES), BF),
                   jax.ShapeDtypeStruct((b, NSA_GROUPS, LANES, ncp), BF)],
        grid=(b,),
        in_specs=[x_spec, x_spec,
                  _const_spec(pek.shape), _const_spec(pev.shape),
                  _const_spec(w1k.shape), _const_spec(w1v.shape),
                  _const_spec(w2k.shape), _const_spec(w2v.shape), _const_spec(cpos.shape)],
        out_specs=[pl.BlockSpec((None, NSA_GROUPS, ncp, LANES), lambda i: (i, 0, 0, 0)),
                   pl.BlockSpec((None, NSA_GROUPS, LANES, ncp), lambda i: (i, 0, 0, 0))],
        compiler_params=pltpu.CompilerParams(
            dimension_semantics=("arbitrary",), vmem_limit_bytes=VMEM_LIMIT),
        name="compress",
    )(xk, xv, pek, pev, w1k, w1v, w2k, w2v, cpos)


def _cmp_kernel(nq_ref, kc_ref, vct_ref, mt_ref, o_ref, sf_ref, ob_ref, *, tq, topk):
    ncp = kc_ref.shape[1]
    ns = mt_ref.shape[0]
    t0 = pl.program_id(1) * tq
    cidx = lax.broadcasted_iota(jnp.int32, (ncp, tq), 0)
    tpos = lax.broadcasted_iota(jnp.int32, (ncp, tq), 1) + t0
    valid = (cidx * CMP_STRIDE + (CMP_LEN - 1)) <= tpos
    lane = lax.broadcasted_iota(jnp.int32, (tq, LANES), 1)
    mt = mt_ref[...]

    blk = lax.broadcasted_iota(jnp.int32, (ns, tq), 0)
    cur = (lax.broadcasted_iota(jnp.int32, (ns, tq), 1) + t0) >> 6
    sub8 = lax.broadcasted_iota(jnp.int32, (8, tq), 0)

    for g in range(NSA_GROUPS):
        kc = kc_ref[g]
        vct = vct_ref[g]
        imp = jnp.zeros((ns, tq), F32)
        for r in range(NSA_REP):
            c = 2 * r + g
            q = nq_ref[:, c * LANES:(c + 1) * LANES]
            s = jnp.where(valid, _dot_nt(kc, q), NEG)
            m = jnp.max(s, axis=0, keepdims=True)
            e = jnp.exp(s - m)
            p = jnp.where(valid, e / jnp.sum(e, axis=0, keepdims=True), 0.0)
            p_hi = p.astype(BF)
            p_lo = (p - p_hi.astype(F32)).astype(BF)
            imp = imp + _dot(mt, p_hi) + _dot(mt, p_lo)
            o = _dot(vct, p_hi).T
            if g == 0:
                ob_ref[r] = o
            else:
                o_ref[:, r * LANES:(r + 1) * LANES] = jnp.where(lane < HALF, ob_ref[r], o)

        score = jnp.where(blk == 0, FORCE_SCORE,
                          jnp.where(blk == cur, FORCE_SCORE,
                                    jnp.where(blk == cur - 1, FORCE_SCORE,
                                              jnp.where(blk <= cur, imp, -1.0))))
        rows = [jnp.broadcast_to(score[i:i + 1, :], (8, tq)) for i in range(ns)]
        feats = []
        for kt in range(ns // 8):
            sc = score[8 * kt:8 * kt + 8, :]
            cnt = jnp.zeros((8, tq), F32)
            for i in range(ns):
                if i < 8 * kt:
                    beat = jnp.where(rows[i] >= sc, 1.0, 0.0)
                elif i >= 8 * kt + 8:
                    beat = jnp.where(rows[i] > sc, 1.0, 0.0)
                else:
                    beat = jnp.where(sub8 + 8 * kt > i,
                                     jnp.where(rows[i] >= sc, 1.0, 0.0),
                                     jnp.where(rows[i] > sc, 1.0, 0.0))
                cnt = cnt + beat
            feats.append(jnp.where(cnt < topk, 0.0, -SEL_MASK_BIG))
        feats.append(jnp.zeros((LANES - ns, tq), F32))
        sf_ref[g] = jnp.concatenate(feats, axis=0).T.astype(BF)


def _cmp_topk(nq, kc, vct, mt, *, tq, topk):
    b, t, _ = nq.shape
    ncp = kc.shape[2]
    return pl.pallas_call(
        functools.partial(_cmp_kernel, tq=tq, topk=topk),
        out_shape=[jax.ShapeDtypeStruct((b, t, NSA_REP * LANES), F32),
                   jax.ShapeDtypeStruct((b, NSA_GROUPS, t, LANES), BF)],
        grid=(b, t // tq),
        in_specs=[
            pl.BlockSpec((None, tq, 2 * NSA_REP * LANES), lambda i, j: (i, j, 0)),
            pl.BlockSpec((None, NSA_GROUPS, ncp, LANES), lambda i, j: (i, 0, 0, 0)),
            pl.BlockSpec((None, NSA_GROUPS, LANES, ncp), lambda i, j: (i, 0, 0, 0)),
            _const_spec(mt.shape),
        ],
        out_specs=[pl.BlockSpec((None, tq, NSA_REP * LANES), lambda i, j: (i, j, 0)),
                   pl.BlockSpec((None, NSA_GROUPS, tq, LANES), lambda i, j: (i, 0, j, 0))],
        scratch_shapes=[pltpu.VMEM((NSA_REP, tq, LANES), F32)],
        compiler_params=pltpu.CompilerParams(
            dimension_semantics=("arbitrary", "arbitrary"), vmem_limit_bytes=VMEM_LIMIT),
        name="cmp_topk",
    )(nq, kc, vct, mt)


def _nsa_flash_kernel(*refs, tq, window, selected):
    if selected:
        nq_ref, sf_ref, k_ref, vt_ref, o_ref, qs_ref, s_ref, acc_ref, m_ref, l_ref, ob_ref = refs
    else:
        nq_ref, k_ref, vt_ref, o_ref, qs_ref, s_ref, acc_ref, m_ref, l_ref, ob_ref = refs
    qi = pl.program_id(1)
    kd = qs_ref.shape[2]
    cols = NSA_REP * tq
    key = lax.broadcasted_iota(jnp.int32, (tq, tq), 0)
    qry = lax.broadcasted_iota(jnp.int32, (tq, tq), 1)
    ones = jnp.ones((BF16_ROWS, tq), BF)

    for g in range(NSA_GROUPS):
        for r in range(NSA_REP):
            c = 2 * r + g
            qs_ref[g, r * tq:(r + 1) * tq, 0:LANES] = nq_ref[:, c * LANES:(c + 1) * LANES]
            if selected:
                qs_ref[g, r * tq:(r + 1) * tq, LANES:2 * LANES] = sf_ref[g]

    m_ref[...] = jnp.full(m_ref.shape, NEG, F32)
    l_ref[...] = jnp.zeros(l_ref.shape, F32)
    acc_ref[...] = jnp.zeros(acc_ref.shape, F32)

    def scores(j, buf):
        ks = pl.multiple_of(j * tq, tq)
        for g in range(NSA_GROUPS):
            s_ref[buf, g] = _dot_nt(k_ref[pl.ds(ks, tq), g * kd:(g + 1) * kd], qs_ref[g])

    def consume(j, buf, mode, live=None):
        if mode == "edge":
            edge_qry = jnp.where(live, qry, tq)
        for g in range(NSA_GROUPS):
            vt = vt_ref[j, g * NSA_HEAD_DIM:(g + 1) * NSA_HEAD_DIM, :]
            for r in range(NSA_REP):
                sl = slice(r * tq, (r + 1) * tq)
                s = s_ref[buf, g, :, sl]
                if mode == "diag":
                    s = jnp.where(key <= qry, s, NEG)
                elif mode == "edge":
                    s = jnp.where(key > edge_qry, s, NEG)
                elif live is not None:
                    s = jnp.where(live, s, NEG)
                m_ref[g, :, sl], l_ref[g, :, sl], acc_ref[g, :, sl] = _softmax_step(
                    s, m_ref[g, :, sl], l_ref[g, :, sl], acc_ref[g, :, sl], vt, ones)

    if window:
        nback = WIN // tq
        for i in range(nback + 1):
            j = qi - nback + i
            scores(jnp.maximum(j, 0), i)
            if i == nback:
                consume(j, i, "diag")
            else:
                consume(jnp.maximum(j, 0), i, "edge" if i == 0 else "full", j >= 0)
    else:
        scores(0, 0)

        def pair(p, carry):
            scores(2 * p + 1, 1)
            consume(2 * p, 0, "full")
            scores(2 * p + 2, 0)
            consume(2 * p + 1, 1, "full")
            return carry

        lax.fori_loop(0, qi // 2, pair, 0)

        @pl.when(qi % 2 == 1)
        def _():
            scores(qi, 1)
            consume(qi - 1, 0, "full")
            consume(qi, 1, "diag")

        @pl.when(qi % 2 == 0)
        def _():
            consume(qi, 0, "diag")

    for g in range(NSA_GROUPS):
        o = acc_ref[g] / l_ref[g]
        for r in range(NSA_REP):
            ob_ref[r, g * NSA_HEAD_DIM:(g + 1) * NSA_HEAD_DIM, :] = o[:, r * tq:(r + 1) * tq]

    for r in range(NSA_REP):
        o_ref[:, r * LANES:(r + 1) * LANES] = ob_ref[r].T


def _nsa_flash(nq, k, vt, sf=None, *, tq, window):
    b, t, _ = nq.shape
    selected = sf is not None
    kd = 2 * LANES if selected else LANES
    nk = t // tq
    assert vt.shape == (b, nk, LANES, tq) and k.shape == (b, t, NSA_GROUPS * kd)
    in_specs = [pl.BlockSpec((None, tq, 2 * NSA_REP * LANES), lambda i, j: (i, j, 0))]
    args = [nq]
    if selected:
        in_specs.append(pl.BlockSpec((None, NSA_GROUPS, tq, LANES), lambda i, j: (i, 0, j, 0)))
        args.append(sf)
    in_specs.append(pl.BlockSpec((None, t, NSA_GROUPS * kd), lambda i, j: (i, 0, 0)))
    args.append(k)
    in_specs.append(pl.BlockSpec((None, nk, LANES, tq), lambda i, j: (i, 0, 0, 0)))
    args.append(vt)
    cols = NSA_REP * tq
    return pl.pallas_call(
        functools.partial(_nsa_flash_kernel, tq=tq, window=window, selected=selected),
        out_shape=jax.ShapeDtypeStruct((b, t, NSA_REP * LANES), F32),
        grid=(b, t // tq),
        in_specs=in_specs,
        out_specs=pl.BlockSpec((None, tq, NSA_REP * LANES), lambda i, j: (i, j, 0)),
        scratch_shapes=[
            pltpu.VMEM((NSA_GROUPS, cols, kd), BF),
            pltpu.VMEM((WIN // tq + 1 if window else 2, NSA_GROUPS, tq, cols), F32),
            pltpu.VMEM((NSA_GROUPS, NSA_HEAD_DIM, cols), F32),
            pltpu.VMEM((NSA_GROUPS, 1, cols), F32),
            pltpu.VMEM((NSA_GROUPS, 1, cols), F32),
            pltpu.VMEM((NSA_REP, LANES, tq), F32),
        ],
        compiler_params=pltpu.CompilerParams(
            dimension_semantics=("arbitrary", "arbitrary"), vmem_limit_bytes=VMEM_LIMIT),
        name="nsa_window" if window else "nsa_selected",
    )(*args)


def _mixout_kernel(x_ref, oa_ref, oc_ref, os_ref, ow_ref, gat_ref, e_ref, wa_ref, wb_ref, post_ref, o_ref):
    gt = gat_ref[...]
    g_hi = gt.astype(BF)
    g_lo = (gt - g_hi.astype(F32)).astype(BF)
    ob = None
    for c, ref in enumerate((oc_ref, os_ref, ow_ref)):
        gate = _dot(g_hi, e_ref[c]) + _dot(g_lo, e_ref[c])
        term = gate * ref[...]
        ob = term if ob is None else ob + term
    y = _dot(oa_ref[...].astype(BF), wa_ref[...]) + _dot(ob.astype(BF), wb_ref[...])
    o_ref[...] = x_ref[...] + _rms(y, post_ref[...])


def _mixout(x2, oa, oc, osel, ow, gates, e, wa, wb, post_g, *, tm):
    n, d = x2.shape
    row = lambda w: pl.BlockSpec((tm, w), lambda i: (i, 0))
    return pl.pallas_call(
        _mixout_kernel,
        out_shape=jax.ShapeDtypeStruct((n, d), F32),
        grid=(n // tm,),
        in_specs=[row(d), row(512), row(512), row(512), row(512), row(LANES),
                  _const_spec(e.shape), _const_spec(wa.shape), _const_spec(wb.shape), _const_spec((1, d))],
        out_specs=row(d),
        compiler_params=pltpu.CompilerParams(
            dimension_semantics=("arbitrary",), vmem_limit_bytes=VMEM_LIMIT),
        name="mixout",
    )(x2, oa, oc, osel, ow, gates, e, wa, wb, post_g)


def _memkv_kernel(m_ref, g_ref, wk_ref, wv_ref, k_ref, v_ref):
    m = _rms(m_ref[...], g_ref[...]).astype(BF)
    k_ref[...] = _dot(m, wk_ref[...]).astype(BF)
    v_ref[...] = _dot(m, wv_ref[...]).astype(BF)


def _memkv(mem, g, wk, wv):
    b, nm, d = mem.shape
    spec = pl.BlockSpec((None, nm, d), lambda i: (i, 0, 0))
    return pl.pallas_call(
        _memkv_kernel,
        out_shape=[jax.ShapeDtypeStruct((b, nm, d), BF)] * 2,
        grid=(b,),
        in_specs=[spec, _const_spec((1, d)), _const_spec(wk.shape), _const_spec(wv.shape)],
        out_specs=[spec, spec],
        compiler_params=pltpu.CompilerParams(
            dimension_semantics=("arbitrary",), vmem_limit_bytes=VMEM_LIMIT),
        name="mem_kv",
    )(mem, g, wk, wv)


def _xa_kernel(x_ref, pre_ref, post_ref, wq_ref, wo_ref, k_ref, v_ref, o_ref, *, scale):
    x = x_ref[...]
    h = _rms(x, pre_ref[...]).astype(BF)
    q = (_dot(h, wq_ref[...]) * scale).astype(BF)
    hd = q.shape[1] // XA_HEADS
    outs = []
    for i in range(XA_HEADS):
        sl = slice(i * hd, (i + 1) * hd)
        s = _dot_nt(q[:, sl], k_ref[:, sl])
        m = jnp.max(s, axis=1, keepdims=True)
        e = jnp.exp(s - m)
        p = e / jnp.sum(e, axis=1, keepdims=True)
        outs.append(_dot(p.astype(BF), v_ref[:, sl]).astype(BF))
    y = _dot(jnp.concatenate(outs, axis=1), wo_ref[...])
    o_ref[...] = x + _rms(y, post_ref[...])


def _xa(x, pre_g, post_g, wq, wo, k, v, *, tm):
    b, t, d = x.shape
    nm = k.shape[1]
    hd = wq.shape[1] // XA_HEADS
    row = pl.BlockSpec((None, tm, d), lambda i, j: (i, j, 0))
    kv = pl.BlockSpec((None, nm, k.shape[2]), lambda i, j: (i, 0, 0))
    return pl.pallas_call(
        functools.partial(_xa_kernel, scale=float(hd) ** -0.5),
        out_shape=jax.ShapeDtypeStruct((b, t, d), F32),
        grid=(b, t // tm),
        in_specs=[row, _const_spec((1, d)), _const_spec((1, d)),
                  _const_spec(wq.shape), _const_spec(wo.shape), kv, kv],
        out_specs=row,
        compiler_params=pltpu.CompilerParams(
            dimension_semantics=("arbitrary", "arbitrary"), vmem_limit_bytes=VMEM_LIMIT),
        name="mem_xattn",
    )(x, pre_g, post_g, wq, wo, k, v)


def _query_feature_rows(slopes_by_chunk):
    out = np.zeros((1, len(slopes_by_chunk) * LANES), np.float32)
    for j, s in enumerate(slopes_by_chunk):
        base = j * LANES + (HALF if j % 2 == 0 else 0)
        out[0, base] = s * 64.0
        out[0, base + 1] = s
    return jnp.asarray(out)


def _static_tables(t):
    ncp = t // CMP_STRIDE
    ns = t // SEL_LEN
    da = _alibi_slopes(DA_HEADS)
    qf_da = _query_feature_rows([da[j // 2] for j in range(2 * DA_HEADS)])
    nsa = _alibi_slopes(NSA_GROUPS * NSA_REP)
    qf_n = _query_feature_rows([nsa[(j % 2) * NSA_REP + j // 2] for j in range(2 * NSA_REP)])

    c_end = np.arange(ncp) * CMP_STRIDE + CMP_LEN - 1
    cpos = np.zeros((NSA_GROUPS, ncp, LANES), np.float32)
    cpos[0, :, HALF] = c_end // 64
    cpos[0, :, HALF + 1] = c_end % 64
    cpos[1, :, 0] = c_end // 64
    cpos[1, :, 1] = c_end % 64

    c_start = np.arange(ncp) * CMP_STRIDE
    s_start = np.arange(ns) * SEL_LEN
    overlap = np.clip(np.minimum(c_start[:, None] + CMP_LEN, s_start[None, :] + SEL_LEN)
                      - np.maximum(c_start[:, None], s_start[None, :]), 0, None)
    mt = (overlap.astype(np.float32) / CMP_LEN).T
    mt[:, ncp - 1] = 0.0

    e = np.zeros((3, LANES, NSA_REP * LANES), np.float32)
    for g in range(NSA_GROUPS):
        for r in range(NSA_REP):
            for c in range(3):
                col = r * LANES + g * HALF
                e[c, g * NSA_REP * 3 + r * 3 + c, col:col + HALF] = 1.0
    return qf_da, qf_n, jnp.asarray(cpos), jnp.asarray(mt, BF), jnp.asarray(e, BF)


def _compress_weights(pe, w1, w2):
    hidden = w1.shape[1]
    w1r = w1.reshape(2, CMP_STRIDE, NSA_HEAD_DIM, hidden)
    w1g = jnp.zeros((2, NSA_GROUPS, CMP_STRIDE, NSA_GROUPS, NSA_HEAD_DIM, hidden), w1.dtype)
    for g in range(NSA_GROUPS):
        w1g = w1g.at[:, g, :, g].set(w1r)
    w1g = w1g.reshape(2, NSA_GROUPS, CMP_STRIDE * LANES, hidden).astype(BF)
    per = pe.reshape(2, CMP_STRIDE, 1, NSA_HEAD_DIM)
    peg = jnp.broadcast_to(per, (2, CMP_STRIDE, NSA_GROUPS, NSA_HEAD_DIM)).reshape(2, 1, CMP_STRIDE * LANES)
    w2g = jnp.zeros((NSA_GROUPS, hidden, NSA_GROUPS, NSA_HEAD_DIM), w2.dtype)
    for g in range(NSA_GROUPS):
        w2g = w2g.at[g, :, g].set(w2)
    w2g = w2g.reshape(NSA_GROUPS, hidden, LANES).astype(BF)
    return peg, w1g, w2g


def kernel(x, mem, ffn1_pre_g, ffn1_post_g, ffn1_w_gate, ffn1_w_up, ffn1_w_down, mix_pre_g, mix_post_g,
           w_mix_in, da_lambda_q1, da_lambda_k1, da_lambda_q2, da_lambda_k2, da_subln_g, cmp_k_pe,
           cmp_k_w1, cmp_k_w2, cmp_v_pe, cmp_v_w1, cmp_v_w2, w_mix_out, xa_pre_g, xa_post_g, mem_norm_g,
           xa_w_q, xa_w_k, xa_w_v, xa_w_o, ffn2_pre_g, ffn2_post_g, ffn2_w_gate, ffn2_w_up, ffn2_w_down):
    b, t, d = x.shape
    depth = ffn1_pre_g.shape[0]
    n = b * t
    tm = 512
    tq_da = 512
    tq = 256
    ns = t // SEL_LEN
    assert t % tm == 0 and ns <= LANES and WIN % tq == 0
    assert d == DA_HEADS * DA_V_DIM + NSA_GROUPS * NSA_REP * NSA_HEAD_DIM
    qf_da, qf_n, cpos, mt, e = _static_tables(t)
    row = lambda v: v.reshape(1, -1)

    for l in range(depth):
        x2 = _ffn(x.reshape(n, d), row(ffn1_pre_g[l]), row(ffn1_post_g[l]),
                  ffn1_w_gate[l].astype(BF), ffn1_w_up[l].astype(BF), ffn1_w_down[l].astype(BF), tm=tm)

        w = w_mix_in[l]
        dq = DA_HEADS * 2 * DA_QK_DIM
        dv = DA_HEADS * DA_V_DIM
        nqw = NSA_GROUPS * NSA_REP * NSA_HEAD_DIM
        nkv = NSA_GROUPS * NSA_HEAD_DIM
        o0 = 2 * dq + dv
        w_nq = w[:, o0:o0 + nqw].reshape(d, NSA_GROUPS, NSA_REP, NSA_HEAD_DIM).transpose(0, 2, 1, 3).reshape(d, nqw)
        o1 = o0 + nqw
        seg = lambda i: w[:, o1 + i * nkv:o1 + (i + 1) * nkv]
        w_g = w[:, o1 + 6 * nkv:]
        w_g = jnp.pad(w_g, ((0, 0), (0, LANES - w_g.shape[1])))
        w_big = jnp.concatenate([w[:, :o0], w_nq, seg(2), seg(4), seg(3), seg(5), seg(0), seg(1), w_g],
                                axis=1).astype(BF)
        (daq, dak, davt, nq, nks, nkw, nvst, nvwt, nkc, nvc, gates) = _mixin(
            x2.reshape(b, t, d), row(mix_pre_g[l]), w_big, qf_da, qf_n, tm=tm, tk_da=tq_da, tk_nsa=tq)

        lam_init = 0.8 - 0.6 * float(np.exp(-0.3 * l))
        o_a = _da_attention(daq, dak, davt, row(da_lambda_q1[l]), row(da_lambda_k1[l]),
                            row(da_lambda_q2[l]), row(da_lambda_k2[l]), row(da_subln_g[l]),
                            lam_init=lam_init, tq=tq_da)

        pek, w1k, w2k = _compress_weights(cmp_k_pe[l], cmp_k_w1[l], cmp_k_w2[l])
        pev, w1v, w2v = _compress_weights(cmp_v_pe[l], cmp_v_w1[l], cmp_v_w2[l])
        ncp = t // CMP_STRIDE
        kc, vct = _compress(nkc.reshape(b, ncp, CMP_STRIDE * LANES), nvc.reshape(b, ncp, CMP_STRIDE * LANES),
                            pek, pev, w1k, w1v, w2k, w2v, cpos)
        o_cmp, sf = _cmp_topk(nq, kc, vct, mt, tq=tq, topk=min(SEL_TOPK, ns))
        o_sel = _nsa_flash(nq, nks, nvst, sf, tq=tq, window=False)
        o_win = _nsa_flash(nq, nkw, nvwt, tq=tq, window=True)

        wo = w_mix_out[l]
        wa = wo[:dv].astype(BF)
        wb = wo[dv:].reshape(NSA_GROUPS, NSA_REP, NSA_HEAD_DIM, d).transpose(1, 0, 2, 3).reshape(nqw, d).astype(BF)
        x3 = _mixout(x2, o_a.reshape(n, dv), o_cmp.reshape(n, nqw), o_sel.reshape(n, nqw),
                     o_win.reshape(n, nqw), gates.reshape(n, LANES), e, wa, wb, row(mix_post_g[l]), tm=tm)

        mk, mv = _memkv(mem, row(mem_norm_g[l]), xa_w_k[l].astype(BF), xa_w_v[l].astype(BF))
        x4 = _xa(x3.reshape(b, t, d), row(xa_pre_g[l]), row(xa_post_g[l]),
                 xa_w_q[l].astype(BF), xa_w_o[l].astype(BF), mk, mv, tm=tm)

        x = _ffn(x4.reshape(n, d), row(ffn2_pre_g[l]), row(ffn2_post_g[l]),
                 ffn2_w_gate[l].astype(BF), ffn2_w_up[l].astype(BF), ffn2_w_down[l].astype(BF),
                 tm=tm).reshape(b, t, d)
    return x
```

```python
import functools

import numpy as np
import jax
import jax.numpy as jnp
from jax import lax
from jax.experimental import pallas as pl
from jax.experimental.pallas import tpu as pltpu

BF = jnp.bfloat16
F32 = jnp.float32

EPS = 1e-6
NEG = -1e30
LANES = 128
HALF = LANES // 2
BF16_ROWS = 16

DA_HEADS = 4
DA_QK_DIM = 64
DA_V_DIM = 128
NSA_GROUPS = 2
NSA_REP = 4
NSA_HEAD_DIM = 64
CMP_LEN = 32
CMP_STRIDE = 16
SEL_LEN = 64
SEL_TOPK = 16
WIN = 512
FORCE_SCORE = 1e4
XA_HEADS = 4
SEL_MASK_BIG = 32768.0

VMEM_LIMIT = 56 * 1024 * 1024

_NT = (((1,), (1,)), ((), ()))


def _rms(x, g):
    ms = jnp.mean(x * x, axis=-1, keepdims=True)
    return x * lax.rsqrt(ms + EPS) * g


def _dot(a, b):
    return jnp.dot(a, b, preferred_element_type=F32)


def _dot_nt(a, b):
    return lax.dot_general(a, b, _NT, preferred_element_type=F32)


def _const_spec(shape):
    nd = len(shape)
    return pl.BlockSpec(shape, lambda *_: (0,) * nd, pipeline_mode=pl.Buffered(1))


def _alibi_slopes(n):
    return [2.0 ** (-8.0 * (i + 1) / n) for i in range(n)]


def _softmax_step(s, m_old, l_old, acc_old, vt, ones):
    m_new = jnp.maximum(m_old, jnp.max(s, axis=0, keepdims=True))
    alpha = jnp.exp(m_old - m_new)
    p = jnp.exp(s - m_new).astype(BF)
    l_new = alpha * l_old + _dot(ones, p)[0:1]
    acc_new = alpha * acc_old + _dot(vt, p)
    return m_new, l_new, acc_new


def _ffn_kernel(x_ref, pre_ref, post_ref, wg_ref, wu_ref, wd_ref, o_ref):
    x = x_ref[...]
    h = _rms(x, pre_ref[...]).astype(BF)
    g = _dot(h, wg_ref[...])
    u = _dot(h, wu_ref[...])
    a = (g * jax.nn.sigmoid(g) * u).astype(BF)
    y = _dot(a, wd_ref[...])
    o_ref[...] = x + 0.5 * _rms(y, post_ref[...])


def _ffn(x2, pre_g, post_g, wg, wu, wd, *, tm):
    n, d = x2.shape
    f = wg.shape[1]
    return pl.pallas_call(
        _ffn_kernel,
        out_shape=jax.ShapeDtypeStruct((n, d), F32),
        grid=(n // tm,),
        in_specs=[
            pl.BlockSpec((tm, d), lambda i: (i, 0)),
            _const_spec((1, d)), _const_spec((1, d)),
            _const_spec((d, f)), _const_spec((d, f)), _const_spec((f, d)),
        ],
        out_specs=pl.BlockSpec((tm, d), lambda i: (i, 0)),
        compiler_params=pltpu.CompilerParams(
            dimension_semantics=("arbitrary",), vmem_limit_bytes=VMEM_LIMIT),
        name="ffn",
    )(x2, pre_g, post_g, wg, wu, wd)


def _mixin_kernel(x_ref, g_ref, w_ref, qfda_ref, qfn_ref,
                  daq, dak, davt, nq, nks, nkw, nvst, nvwt, nkc, nvc, gat, *, tk_da, tk_nsa):
    tm = x_ref.shape[0]
    t0 = pl.program_id(1) * tm
    h = _rms(x_ref[...], g_ref[...]).astype(BF)
    z = _dot(h, w_ref[...])

    lane = lax.broadcasted_iota(jnp.int32, (tm, LANES), 1)
    row = lax.broadcasted_iota(jnp.int32, (tm, LANES), 0) + t0
    pa = (row >> 6).astype(F32)
    pb = (row & 63).astype(F32)
    low = lane < HALF
    pos_lo = jnp.where(lane == HALF, pa, jnp.where(lane == HALF + 1, pb, 0.0))
    pos_hi = jnp.where(lane == 0, pa, jnp.where(lane == 1, pb, 0.0))
    blk_onehot = jnp.where(lane == (row >> 6), 1.0, 0.0)

    def ch(i):
        return z[:, i * LANES:(i + 1) * LANES]

    def put(ref, j, val):
        ref[:, j * LANES:(j + 1) * LANES] = val.astype(ref.dtype)

    for c in range(4):
        q = ch(c) * 0.125
        put(daq, 2 * c, jnp.where(low, q, qfda_ref[:, (2 * c) * LANES:(2 * c + 1) * LANES]))
        put(daq, 2 * c + 1, jnp.where(low, qfda_ref[:, (2 * c + 1) * LANES:(2 * c + 2) * LANES], q))
        k = ch(4 + c)
        put(dak, 2 * c, jnp.where(low, k, pos_lo))
        put(dak, 2 * c + 1, jnp.where(low, pos_hi, k))
        vt = ch(8 + c).T
        for s in range(tm // tk_da):
            davt[s, c * LANES:(c + 1) * LANES, :] = vt[:, s * tk_da:(s + 1) * tk_da].astype(BF)
        q = ch(12 + c) * 0.125
        put(nq, 2 * c, jnp.where(low, q, qfn_ref[:, (2 * c) * LANES:(2 * c + 1) * LANES]))
        put(nq, 2 * c + 1, jnp.where(low, qfn_ref[:, (2 * c + 1) * LANES:(2 * c + 2) * LANES], q))
    k = ch(16)
    put(nks, 0, jnp.where(low, k, pos_lo))
    put(nks, 1, blk_onehot)
    put(nks, 2, jnp.where(low, pos_hi, k))
    put(nks, 3, blk_onehot)
    k = ch(17)
    put(nkw, 0, jnp.where(low, k, pos_lo))
    put(nkw, 1, jnp.where(low, pos_hi, k))
    for src, ref in ((18, nvst), (19, nvwt)):
        vt = ch(src).T
        for s in range(tm // tk_nsa):
            ref[s] = vt[:, s * tk_nsa:(s + 1) * tk_nsa].astype(BF)
    nkc[...] = ch(20)
    nvc[...] = ch(21)
    gat[...] = jax.nn.sigmoid(ch(22))


def _mixin(x, g, w, qf_da, qf_n, *, tm, tk_da, tk_nsa):
    b, t, d = x.shape
    wcols = w.shape[1]
    assert tm % tk_da == 0 and tm % tk_nsa == 0
    row = lambda wd, dt: (jax.ShapeDtypeStruct((b, t, wd), dt),
                          pl.BlockSpec((None, tm, wd), lambda i, j: (i, j, 0)))
    tr = lambda rows, tk: (jax.ShapeDtypeStruct((b, t // tk, rows, tk), BF),
                           pl.BlockSpec((None, tm // tk, rows, tk), lambda i, j: (i, j, 0, 0)))
    outs = [row(1024, BF), row(1024, BF), tr(512, tk_da), row(1024, BF), row(512, BF), row(256, BF),
            tr(LANES, tk_nsa), tr(LANES, tk_nsa), row(LANES, F32), row(LANES, F32), row(LANES, F32)]
    return pl.pallas_call(
        functools.partial(_mixin_kernel, tk_da=tk_da, tk_nsa=tk_nsa),
        out_shape=[o[0] for o in outs],
        grid=(b, t // tm),
        in_specs=[
            pl.BlockSpec((None, tm, d), lambda i, j: (i, j, 0)),
            _const_spec((1, d)), _const_spec((d, wcols)),
            _const_spec((1, 1024)), _const_spec((1, 1024)),
        ],
        out_specs=[o[1] for o in outs],
        compiler_params=pltpu.CompilerParams(
            dimension_semantics=("arbitrary", "arbitrary"), vmem_limit_bytes=VMEM_LIMIT),
        name="mixin",
    )(x, g, w, qf_da, qf_n)


def _da_kernel(q_ref, k_ref, vt_ref, lq1, lk1, lq2, lk2, sg_ref, o_ref, s_ref, acc_ref, m_ref, l_ref,
               *, lam_init, tq, strip):
    qi = pl.program_id(2)
    m_ref[...] = jnp.full(m_ref.shape, NEG, F32)
    l_ref[...] = jnp.zeros(l_ref.shape, F32)
    acc_ref[...] = jnp.zeros(acc_ref.shape, F32)
    key = lax.broadcasted_iota(jnp.int32, (tq, strip), 0)
    qry = lax.broadcasted_iota(jnp.int32, (tq, strip), 1)
    ones = jnp.ones((BF16_ROWS, tq), BF)

    def scores(j, buf):
        ks = pl.multiple_of(j * tq, tq)
        for mp in range(2):
            s_ref[buf, mp] = _dot_nt(k_ref[pl.ds(ks, tq), mp * LANES:(mp + 1) * LANES],
                                     q_ref[:, mp * LANES:(mp + 1) * LANES])

    def consume(j, buf, masked):
        vt = vt_ref[j]
        for mp in range(2):
            for c in range(tq // strip):
                sl = slice(c * strip, (c + 1) * strip)
                s = s_ref[buf, mp, :, sl]
                if masked:
                    s = jnp.where(key <= qry + c * strip, s, NEG)
                m_ref[mp, :, sl], l_ref[mp, :, sl], acc_ref[mp, :, sl] = _softmax_step(
                    s, m_ref[mp, :, sl], l_ref[mp, :, sl], acc_ref[mp, :, sl], vt, ones)

    scores(0, 0)

    def pair(p, carry):
        scores(2 * p + 1, 1)
        consume(2 * p, 0, False)
        scores(2 * p + 2, 0)
        consume(2 * p + 1, 1, False)
        return carry

    lax.fori_loop(0, qi // 2, pair, 0)

    @pl.when(qi % 2 == 1)
    def _():
        scores(qi, 1)
        consume(qi - 1, 0, False)
        consume(qi, 1, True)

    @pl.when(qi % 2 == 0)
    def _():
        consume(qi, 0, True)

    lam = (jnp.exp(jnp.sum(lq1[...] * lk1[...], axis=1, keepdims=True))
           - jnp.exp(jnp.sum(lq2[...] * lk2[...], axis=1, keepdims=True)) + lam_init)
    o = (acc_ref[0] / l_ref[0] - lam * (acc_ref[1] / l_ref[1])).T
    o_ref[...] = (_rms(o, sg_ref[...]) * (1.0 - lam_init)).astype(o_ref.dtype)


def _da_attention(daq, dak, davt, lq1, lk1, lq2, lk2, subln_g, *, lam_init, tq):
    b, t, _ = daq.shape
    nk = t // tq
    vec = _const_spec((1, DA_QK_DIM))
    return pl.pallas_call(
        functools.partial(_da_kernel, lam_init=lam_init, tq=tq, strip=2 * LANES),
        out_shape=jax.ShapeDtypeStruct((b, t, DA_HEADS * DA_V_DIM), BF),
        grid=(b, DA_HEADS, t // tq),
        in_specs=[
            pl.BlockSpec((None, tq, 2 * LANES), lambda i, h, j: (i, j, h)),
            pl.BlockSpec((None, t, 2 * LANES), lambda i, h, j: (i, 0, h)),
            pl.BlockSpec((None, nk, DA_V_DIM, tq), lambda i, h, j: (i, 0, h, 0)),
            vec, vec, vec, vec, _const_spec((1, DA_V_DIM)),
        ],
        out_specs=pl.BlockSpec((None, tq, DA_V_DIM), lambda i, h, j: (i, j, h)),
        scratch_shapes=[
            pltpu.VMEM((2, 2, tq, tq), F32),
            pltpu.VMEM((2, DA_V_DIM, tq), F32),
            pltpu.VMEM((2, 1, tq), F32),
            pltpu.VMEM((2, 1, tq), F32),
        ],
        compiler_params=pltpu.CompilerParams(
            dimension_semantics=("arbitrary", "arbitrary", "arbitrary"), vmem_limit_bytes=VMEM_LIMIT),
        name="diff_attn",
    )(daq, dak, davt, lq1, lk1, lq2, lk2, subln_g)


def _compress_kernel(xk_ref, xv_ref, pek_ref, pev_ref, w1k_ref, w1v_ref, w2k_ref, w2v_ref, cpos_ref,
                     kc_ref, vct_ref):
    ncp = xk_ref.shape[0]
    for x_ref, pe_ref, w1_ref, w2_ref, is_v in ((xk_ref, pek_ref, w1k_ref, w2k_ref, False),
                                                (xv_ref, pev_ref, w1v_ref, w2v_ref, True)):
        x = x_ref[...]
        xa = (x + pe_ref[0]).astype(BF)
        xb = (x + pe_ref[1]).astype(BF)
        for g in range(NSA_GROUPS):
            first = _dot(xa, w1_ref[0, g])
            second = _dot(xb, w1_ref[1, g])
            hid = first + pltpu.roll(second, ncp - 1, 0)
            act = (hid * jax.nn.sigmoid(hid)).astype(BF)
            out = _dot(act, w2_ref[g])
            if is_v:
                vct_ref[g] = out.T.astype(BF)
            else:
                kc_ref[g] = (out + cpos_ref[g]).astype(BF)


def _compress(xk, xv, pek, pev, w1k, w1v, w2k, w2v, cpos):
    b, ncp, width = xk.shape
    x_spec = pl.BlockSpec((None, ncp, width), lambda i: (i, 0, 0))
    return pl.pallas_call(
        _compress_kernel,
        out_shape=[jax.ShapeDtypeStruct((b, NSA_GROUPS, ncp, LANES), BF),
                   jax.ShapeDtypeStruct((b, NSA_GROUPS, LANES, ncp), BF)],
        grid=(b,),
        in_specs=[x_spec, x_spec,
                  _const_spec(pek.shape), _const_spec(pev.shape),
                  _const_spec(w1k.shape), _const_spec(w1v.shape),
                  _const_spec(w2k.shape), _const_spec(w2v.shape), _const_spec(cpos.shape)],
        out_specs=[pl.BlockSpec((None, NSA_GROUPS, ncp, LANES), lambda i: (i, 0, 0, 0)),
                   pl.BlockSpec((None, NSA_GROUPS, LANES, ncp), lambda i: (i, 0, 0, 0))],
        compiler_params=pltpu.CompilerParams(
            dimension_semantics=("arbitrary",), vmem_limit_bytes=VMEM_LIMIT),
        name="compress",
    )(xk, xv, pek, pev, w1k, w1v, w2k, w2v, cpos)


def _cmp_kernel(nq_ref, kc_ref, vct_ref, mt_ref, o_ref, sf_ref, ob_ref, *, tq, topk):
    ncp = kc_ref.shape[1]
    ns = mt_ref.shape[0]
    t0 = pl.program_id(1) * tq
    cidx = lax.broadcasted_iota(jnp.int32, (ncp, tq), 0)
    tpos = lax.broadcasted_iota(jnp.int32, (ncp, tq), 1) + t0
    valid = (cidx * CMP_STRIDE + (CMP_LEN - 1)) <= tpos
    col_live = (lax.broadcasted_iota(jnp.int32, (1, tq), 1) + t0) >= CMP_LEN - 1
    lane = lax.broadcasted_iota(jnp.int32, (tq, LANES), 1)
    mt = mt_ref[...]

    blk = lax.broadcasted_iota(jnp.int32, (ns, tq), 0)
    cur = (lax.broadcasted_iota(jnp.int32, (ns, tq), 1) + t0) >> 6
    sub8 = lax.broadcasted_iota(jnp.int32, (8, tq), 0)

    for g in range(NSA_GROUPS):
        kc = kc_ref[g]
        vct = vct_ref[g]
        imp = jnp.zeros((ns, tq), F32)
        for r in range(NSA_REP):
            c = 2 * r + g
            q = nq_ref[:, c * LANES:(c + 1) * LANES]
            s = jnp.where(valid, _dot_nt(kc, q), NEG)
            m = jnp.max(s, axis=0, keepdims=True)
            e = jnp.exp(s - m)
            p = e * jnp.where(col_live, 1.0 / jnp.sum(e, axis=0, keepdims=True), 0.0)
            p_hi = p.astype(BF)
            p_lo = (p - p_hi.astype(F32)).astype(BF)
            imp = imp + _dot(mt, p_hi) + _dot(mt, p_lo)
            o = _dot(vct, p_hi).T
            if g == 0:
                ob_ref[r] = o
            else:
                o_ref[:, r * LANES:(r + 1) * LANES] = jnp.where(lane < HALF, ob_ref[r], o).astype(o_ref.dtype)

        score = jnp.where(blk == 0, FORCE_SCORE,
                          jnp.where(blk == cur, FORCE_SCORE,
                                    jnp.where(blk == cur - 1, FORCE_SCORE,
                                              jnp.where(blk <= cur, imp, -1.0))))
        rows = [jnp.broadcast_to(score[i:i + 1, :], (8, tq)) for i in range(ns)]
        feats = []
        for kt in range(ns // 8):
            sc = score[8 * kt:8 * kt + 8, :]
            cnt = jnp.zeros((8, tq), F32)
            for i in range(ns):
                if i < 8 * kt:
                    beat = jnp.where(rows[i] >= sc, 1.0, 0.0)
                elif i >= 8 * kt + 8:
                    beat = jnp.where(rows[i] > sc, 1.0, 0.0)
                else:
                    beat = jnp.where(sub8 + 8 * kt > i,
                                     jnp.where(rows[i] >= sc, 1.0, 0.0),
                                     jnp.where(rows[i] > sc, 1.0, 0.0))
                cnt = cnt + beat
            feats.append(jnp.where(cnt < topk, 0.0, -SEL_MASK_BIG))
        feats.append(jnp.zeros((LANES - ns, tq), F32))
        sf_ref[g] = jnp.concatenate(feats, axis=0).T.astype(BF)


def _cmp_topk(nq, kc, vct, mt, *, tq, topk):
    b, t, _ = nq.shape
    ncp = kc.shape[2]
    return pl.pallas_call(
        functools.partial(_cmp_kernel, tq=tq, topk=topk),
        out_shape=[jax.ShapeDtypeStruct((b, t, NSA_REP * LANES), BF),
                   jax.ShapeDtypeStruct((b, NSA_GROUPS, t, LANES), BF)],
        grid=(b, t // tq),
        in_specs=[
            pl.BlockSpec((None, tq, 2 * NSA_REP * LANES), lambda i, j: (i, j, 0)),
            pl.BlockSpec((None, NSA_GROUPS, ncp, LANES), lambda i, j: (i, 0, 0, 0)),
            pl.BlockSpec((None, NSA_GROUPS, LANES, ncp), lambda i, j: (i, 0, 0, 0)),
            _const_spec(mt.shape),
        ],
        out_specs=[pl.BlockSpec((None, tq, NSA_REP * LANES), lambda i, j: (i, j, 0)),
                   pl.BlockSpec((None, NSA_GROUPS, tq, LANES), lambda i, j: (i, 0, j, 0))],
        scratch_shapes=[pltpu.VMEM((NSA_REP, tq, LANES), F32)],
        compiler_params=pltpu.CompilerParams(
            dimension_semantics=("arbitrary", "arbitrary"), vmem_limit_bytes=VMEM_LIMIT),
        name="cmp_topk",
    )(nq, kc, vct, mt)


def _nsa_flash_kernel(*refs, tq, window, selected):
    if selected:
        nq_ref, sf_ref, k_ref, vt_ref, o_ref, qs_ref, s_ref, acc_ref, m_ref, l_ref, ob_ref = refs
    else:
        nq_ref, k_ref, vt_ref, o_ref, qs_ref, s_ref, acc_ref, m_ref, l_ref, ob_ref = refs
    qi = pl.program_id(1)
    kd = qs_ref.shape[2]
    cols = NSA_REP * tq
    key = lax.broadcasted_iota(jnp.int32, (tq, tq), 0)
    qry = lax.broadcasted_iota(jnp.int32, (tq, tq), 1)
    ones = jnp.ones((BF16_ROWS, tq), BF)

    for g in range(NSA_GROUPS):
        for r in range(NSA_REP):
            c = 2 * r + g
            qs_ref[g, r * tq:(r + 1) * tq, 0:LANES] = nq_ref[:, c * LANES:(c + 1) * LANES]
            if selected:
                qs_ref[g, r * tq:(r + 1) * tq, LANES:2 * LANES] = sf_ref[g]

    m_ref[...] = jnp.full(m_ref.shape, NEG, F32)
    l_ref[...] = jnp.zeros(l_ref.shape, F32)
    acc_ref[...] = jnp.zeros(acc_ref.shape, F32)

    def scores(j, buf):
        ks = pl.multiple_of(j * tq, tq)
        for g in range(NSA_GROUPS):
            s_ref[buf, g] = _dot_nt(k_ref[pl.ds(ks, tq), g * kd:(g + 1) * kd], qs_ref[g])

    def consume(j, buf, mode, live=None):
        if mode == "edge":
            edge_qry = jnp.where(live, qry, tq)
        for g in range(NSA_GROUPS):
            vt = vt_ref[j, g * NSA_HEAD_DIM:(g + 1) * NSA_HEAD_DIM, :]
            for r in range(NSA_REP):
                sl = slice(r * tq, (r + 1) * tq)
                s = s_ref[buf, g, :, sl]
                if mode == "diag":
                    s = jnp.where(key <= qry, s, NEG)
                elif mode == "edge":
                    s = jnp.where(key > edge_qry, s, NEG)
                elif live is not None:
                    s = jnp.where(live, s, NEG)
                m_ref[g, :, sl], l_ref[g, :, sl], acc_ref[g, :, sl] = _softmax_step(
                    s, m_ref[g, :, sl], l_ref[g, :, sl], acc_ref[g, :, sl], vt, ones)

    if window:
        nback = WIN // tq
        for i in range(nback + 1):
            j = qi - nback + i
            scores(jnp.maximum(j, 0), i)
            if i == nback:
                consume(j, i, "diag")
            else:
                consume(jnp.maximum(j, 0), i, "edge" if i == 0 else "full", j >= 0)
    else:
        scores(0, 0)

        def pair(p, carry):
            scores(2 * p + 1, 1)
            consume(2 * p, 0, "full")
            scores(2 * p + 2, 0)
            consume(2 * p + 1, 1, "full")
            return carry

        lax.fori_loop(0, qi // 2, pair, 0)

        @pl.when(qi % 2 == 1)
        def _():
            scores(qi, 1)
            consume(qi - 1, 0, "full")
            consume(qi, 1, "diag")

        @pl.when(qi % 2 == 0)
        def _():
            consume(qi, 0, "diag")

    for g in range(NSA_GROUPS):
        o = acc_ref[g] / l_ref[g]
        for r in range(NSA_REP):
            ob_ref[r, g * NSA_HEAD_DIM:(g + 1) * NSA_HEAD_DIM, :] = o[:, r * tq:(r + 1) * tq]

    for r in range(NSA_REP):
        o_ref[:, r * LANES:(r + 1) * LANES] = ob_ref[r].T.astype(o_ref.dtype)


def _nsa_flash(nq, k, vt, sf=None, *, tq, window):
    b, t, _ = nq.shape
    selected = sf is not None
    kd = 2 * LANES if selected else LANES
    nk = t // tq
    assert vt.shape == (b, nk, LANES, tq) and k.shape == (b, t, NSA_GROUPS * kd)
    in_specs = [pl.BlockSpec((None, tq, 2 * NSA_REP * LANES), lambda i, j: (i, j, 0))]
    args = [nq]
    if selected:
        in_specs.append(pl.BlockSpec((None, NSA_GROUPS, tq, LANES), lambda i, j: (i, 0, j, 0)))
        args.append(sf)
    in_specs.append(pl.BlockSpec((None, t, NSA_GROUPS * kd), lambda i, j: (i, 0, 0)))
    args.append(k)
    in_specs.append(pl.BlockSpec((None, nk, LANES, tq), lambda i, j: (i, 0, 0, 0)))
    args.append(vt)
    cols = NSA_REP * tq
    return pl.pallas_call(
        functools.partial(_nsa_flash_kernel, tq=tq, window=window, selected=selected),
        out_shape=jax.ShapeDtypeStruct((b, t, NSA_REP * LANES), BF),
        grid=(b, t // tq),
        in_specs=in_specs,
        out_specs=pl.BlockSpec((None, tq, NSA_REP * LANES), lambda i, j: (i, j, 0)),
        scratch_shapes=[
            pltpu.VMEM((NSA_GROUPS, cols, kd), BF),
            pltpu.VMEM((WIN // tq + 1 if window else 2, NSA_GROUPS, tq, cols), F32),
            pltpu.VMEM((NSA_GROUPS, NSA_HEAD_DIM, cols), F32),
            pltpu.VMEM((NSA_GROUPS, 1, cols), F32),
            pltpu.VMEM((NSA_GROUPS, 1, cols), F32),
            pltpu.VMEM((NSA_REP, LANES, tq), F32),
        ],
        compiler_params=pltpu.CompilerParams(
            dimension_semantics=("arbitrary", "arbitrary"), vmem_limit_bytes=VMEM_LIMIT),
        name="nsa_window" if window else "nsa_selected",
    )(*args)


def _mixout_kernel(x_ref, oa_ref, oc_ref, os_ref, ow_ref, gat_ref, e_ref, wa_ref, wb_ref, post_ref, o_ref):
    gt = gat_ref[...]
    g_hi = gt.astype(BF)
    g_lo = (gt - g_hi.astype(F32)).astype(BF)
    ob = None
    for c, ref in enumerate((oc_ref, os_ref, ow_ref)):
        gate = _dot(g_hi, e_ref[c]) + _dot(g_lo, e_ref[c])
        term = gate * ref[...].astype(F32)
        ob = term if ob is None else ob + term
    y = _dot(oa_ref[...], wa_ref[...]) + _dot(ob.astype(BF), wb_ref[...])
    o_ref[...] = x_ref[...] + _rms(y, post_ref[...])


def _mixout(x2, oa, oc, osel, ow, gates, e, wa, wb, post_g, *, tm):
    n, d = x2.shape
    row = lambda w: pl.BlockSpec((tm, w), lambda i: (i, 0))
    return pl.pallas_call(
        _mixout_kernel,
        out_shape=jax.ShapeDtypeStruct((n, d), F32),
        grid=(n // tm,),
        in_specs=[row(d), row(512), row(512), row(512), row(512), row(LANES),
                  _const_spec(e.shape), _const_spec(wa.shape), _const_spec(wb.shape), _const_spec((1, d))],
        out_specs=row(d),
        compiler_params=pltpu.CompilerParams(
            dimension_semantics=("arbitrary",), vmem_limit_bytes=VMEM_LIMIT),
        name="mixout",
    )(x2, oa, oc, osel, ow, gates, e, wa, wb, post_g)


def _memkv_kernel(m_ref, g_ref, wk_ref, wv_ref, k_ref, v_ref):
    m = _rms(m_ref[...], g_ref[...]).astype(BF)
    k_ref[...] = _dot(m, wk_ref[...]).astype(BF)
    v_ref[...] = _dot(m, wv_ref[...]).astype(BF)


def _memkv(mem, g, wk, wv):
    b, nm, d = mem.shape
    spec = pl.BlockSpec((None, nm, d), lambda i: (i, 0, 0))
    return pl.pallas_call(
        _memkv_kernel,
        out_shape=[jax.ShapeDtypeStruct((b, nm, d), BF)] * 2,
        grid=(b,),
        in_specs=[spec, _const_spec((1, d)), _const_spec(wk.shape), _const_spec(wv.shape)],
        out_specs=[spec, spec],
        compiler_params=pltpu.CompilerParams(
            dimension_semantics=("arbitrary",), vmem_limit_bytes=VMEM_LIMIT),
        name="mem_kv",
    )(mem, g, wk, wv)


def _xa_kernel(x_ref, pre_ref, post_ref, wq_ref, wo_ref, k_ref, v_ref, o_ref, *, scale):
    x = x_ref[...]
    h = _rms(x, pre_ref[...]).astype(BF)
    q = (_dot(h, wq_ref[...]) * scale).astype(BF)
    hd = q.shape[1] // XA_HEADS
    outs = []
    for i in range(XA_HEADS):
        sl = slice(i * hd, (i + 1) * hd)
        s = _dot_nt(q[:, sl], k_ref[:, sl])
        m = jnp.max(s, axis=1, keepdims=True)
        e = jnp.exp(s - m)
        p = e / jnp.sum(e, axis=1, keepdims=True)
        outs.append(_dot(p.astype(BF), v_ref[:, sl]).astype(BF))
    y = _dot(jnp.concatenate(outs, axis=1), wo_ref[...])
    o_ref[...] = x + _rms(y, post_ref[...])


def _xa(x, pre_g, post_g, wq, wo, k, v, *, tm):
    b, t, d = x.shape
    nm = k.shape[1]
    hd = wq.shape[1] // XA_HEADS
    row = pl.BlockSpec((None, tm, d), lambda i, j: (i, j, 0))
    kv = pl.BlockSpec((None, nm, k.shape[2]), lambda i, j: (i, 0, 0))
    return pl.pallas_call(
        functools.partial(_xa_kernel, scale=float(hd) ** -0.5),
        out_shape=jax.ShapeDtypeStruct((b, t, d), F32),
        grid=(b, t // tm),
        in_specs=[row, _const_spec((1, d)), _const_spec((1, d)),
                  _const_spec(wq.shape), _const_spec(wo.shape), kv, kv],
        out_specs=row,
        compiler_params=pltpu.CompilerParams(
            dimension_semantics=("arbitrary", "arbitrary"), vmem_limit_bytes=VMEM_LIMIT),
        name="mem_xattn",
    )(x, pre_g, post_g, wq, wo, k, v)


def _query_feature_rows(slopes_by_chunk):
    out = np.zeros((1, len(slopes_by_chunk) * LANES), np.float32)
    for j, s in enumerate(slopes_by_chunk):
        base = j * LANES + (HALF if j % 2 == 0 else 0)
        out[0, base] = s * 64.0
        out[0, base + 1] = s
    return jnp.asarray(out)


def _static_tables(t):
    ncp = t // CMP_STRIDE
    ns = t // SEL_LEN
    da = _alibi_slopes(DA_HEADS)
    qf_da = _query_feature_rows([da[j // 2] for j in range(2 * DA_HEADS)])
    nsa = _alibi_slopes(NSA_GROUPS * NSA_REP)
    qf_n = _query_feature_rows([nsa[(j % 2) * NSA_REP + j // 2] for j in range(2 * NSA_REP)])

    c_end = np.arange(ncp) * CMP_STRIDE + CMP_LEN - 1
    cpos = np.zeros((NSA_GROUPS, ncp, LANES), np.float32)
    cpos[0, :, HALF] = c_end // 64
    cpos[0, :, HALF + 1] = c_end % 64
    cpos[1, :, 0] = c_end // 64
    cpos[1, :, 1] = c_end % 64

    c_start = np.arange(ncp) * CMP_STRIDE
    s_start = np.arange(ns) * SEL_LEN
    overlap = np.clip(np.minimum(c_start[:, None] + CMP_LEN, s_start[None, :] + SEL_LEN)
                      - np.maximum(c_start[:, None], s_start[None, :]), 0, None)
    mt = (overlap.astype(np.float32) / CMP_LEN).T
    mt[:, ncp - 1] = 0.0

    e = np.zeros((3, LANES, NSA_REP * LANES), np.float32)
    for g in range(NSA_GROUPS):
        for r in range(NSA_REP):
            for c in range(3):
                col = r * LANES + g * HALF
                e[c, g * NSA_REP * 3 + r * 3 + c, col:col + HALF] = 1.0
    return qf_da, qf_n, jnp.asarray(cpos), jnp.asarray(mt, BF), jnp.asarray(e, BF)


def _compress_weights(pe, w1, w2):
    hidden = w1.shape[1]
    w1r = w1.reshape(2, CMP_STRIDE, NSA_HEAD_DIM, hidden)
    w1g = jnp.zeros((2, NSA_GROUPS, CMP_STRIDE, NSA_GROUPS, NSA_HEAD_DIM, hidden), w1.dtype)
    for g in range(NSA_GROUPS):
        w1g = w1g.at[:, g, :, g].set(w1r)
    w1g = w1g.reshape(2, NSA_GROUPS, CMP_STRIDE * LANES, hidden).astype(BF)
    per = pe.reshape(2, CMP_STRIDE, 1, NSA_HEAD_DIM)
    peg = jnp.broadcast_to(per, (2, CMP_STRIDE, NSA_GROUPS, NSA_HEAD_DIM)).reshape(2, 1, CMP_STRIDE * LANES)
    w2g = jnp.zeros((NSA_GROUPS, hidden, NSA_GROUPS, NSA_HEAD_DIM), w2.dtype)
    for g in range(NSA_GROUPS):
        w2g = w2g.at[g, :, g].set(w2)
    w2g = w2g.reshape(NSA_GROUPS, hidden, LANES).astype(BF)
    return peg, w1g, w2g


def kernel(x, mem, ffn1_pre_g, ffn1_post_g, ffn1_w_gate, ffn1_w_up, ffn1_w_down, mix_pre_g, mix_post_g,
           w_mix_in, da_lambda_q1, da_lambda_k1, da_lambda_q2, da_lambda_k2, da_subln_g, cmp_k_pe,
           cmp_k_w1, cmp_k_w2, cmp_v_pe, cmp_v_w1, cmp_v_w2, w_mix_out, xa_pre_g, xa_post_g, mem_norm_g,
           xa_w_q, xa_w_k, xa_w_v, xa_w_o, ffn2_pre_g, ffn2_post_g, ffn2_w_gate, ffn2_w_up, ffn2_w_down):
    b, t, d = x.shape
    depth = ffn1_pre_g.shape[0]
    n = b * t
    tm = 512
    tq_da = 512
    tq = 256
    ns = t // SEL_LEN
    assert t % tm == 0 and ns <= LANES and WIN % tq == 0
    assert d == DA_HEADS * DA_V_DIM + NSA_GROUPS * NSA_REP * NSA_HEAD_DIM
    qf_da, qf_n, cpos, mt, e = _static_tables(t)
    row = lambda v: v.reshape(1, -1)

    for l in range(depth):
        x2 = _ffn(x.reshape(n, d), row(ffn1_pre_g[l]), row(ffn1_post_g[l]),
                  ffn1_w_gate[l].astype(BF), ffn1_w_up[l].astype(BF), ffn1_w_down[l].astype(BF), tm=tm)

        w = w_mix_in[l]
        dq = DA_HEADS * 2 * DA_QK_DIM
        dv = DA_HEADS * DA_V_DIM
        nqw = NSA_GROUPS * NSA_REP * NSA_HEAD_DIM
        nkv = NSA_GROUPS * NSA_HEAD_DIM
        o0 = 2 * dq + dv
        w_nq = w[:, o0:o0 + nqw].reshape(d, NSA_GROUPS, NSA_REP, NSA_HEAD_DIM).transpose(0, 2, 1, 3).reshape(d, nqw)
        o1 = o0 + nqw
        seg = lambda i: w[:, o1 + i * nkv:o1 + (i + 1) * nkv]
        w_g = w[:, o1 + 6 * nkv:]
        w_g = jnp.pad(w_g, ((0, 0), (0, LANES - w_g.shape[1])))
        w_big = jnp.concatenate([w[:, :o0], w_nq, seg(2), seg(4), seg(3), seg(5), seg(0), seg(1), w_g],
                                axis=1).astype(BF)
        (daq, dak, davt, nq, nks, nkw, nvst, nvwt, nkc, nvc, gates) = _mixin(
            x2.reshape(b, t, d), row(mix_pre_g[l]), w_big, qf_da, qf_n, tm=tm, tk_da=tq_da, tk_nsa=tq)

        lam_init = 0.8 - 0.6 * float(np.exp(-0.3 * l))
        o_a = _da_attention(daq, dak, davt, row(da_lambda_q1[l]), row(da_lambda_k1[l]),
                            row(da_lambda_q2[l]), row(da_lambda_k2[l]), row(da_subln_g[l]),
                            lam_init=lam_init, tq=tq_da)

        pek, w1k, w2k = _compress_weights(cmp_k_pe[l], cmp_k_w1[l], cmp_k_w2[l])
        pev, w1v, w2v = _compress_weights(cmp_v_pe[l], cmp_v_w1[l], cmp_v_w2[l])
        ncp = t // CMP_STRIDE
        kc, vct = _compress(nkc.reshape(b, ncp, CMP_STRIDE * LANES), nvc.reshape(b, ncp, CMP_STRIDE * LANES),
                            pek, pev, w1k, w1v, w2k, w2v, cpos)
        o_cmp, sf = _cmp_topk(nq, kc, vct, mt, tq=tq, topk=min(SEL_TOPK, ns))
        o_sel = _nsa_flash(nq, nks, nvst, sf, tq=tq, window=False)
        o_win = _nsa_flash(nq, nkw, nvwt, tq=tq, window=True)

        wo = w_mix_out[l]
        wa = wo[:dv].astype(BF)
        wb = wo[dv:].reshape(NSA_GROUPS, NSA_REP, NSA_HEAD_DIM, d).transpose(1, 0, 2, 3).reshape(nqw, d).astype(BF)
        x3 = _mixout(x2, o_a.reshape(n, dv), o_cmp.reshape(n, nqw), o_sel.reshape(n, nqw),
                     o_win.reshape(n, nqw), gates.reshape(n, LANES), e, wa, wb, row(mix_post_g[l]), tm=tm)

        mk, mv = _memkv(mem, row(mem_norm_g[l]), xa_w_k[l].astype(BF), xa_w_v[l].astype(BF))
        x4 = _xa(x3.reshape(b, t, d), row(xa_pre_g[l]), row(xa_post_g[l]),
                 xa_w_q[l].astype(BF), xa_w_o[l].astype(BF), mk, mv, tm=tm)

        x = _ffn(x4.reshape(n, d), row(ffn2_pre_g[l]), row(ffn2_post_g[l]),
                 ffn2_w_gate[l].astype(BF), ffn2_w_up[l].astype(BF), ffn2_w_down[l].astype(BF),
                 tm=tm).reshape(b, t, d)
    return x
```

```python
import functools

import numpy as np
import jax
import jax.numpy as jnp
from jax import lax
from jax.experimental import pallas as pl
from jax.experimental.pallas import tpu as pltpu

BF = jnp.bfloat16
F32 = jnp.float32

EPS = 1e-6
NEG = -1e30
LANES = 128
HALF = LANES // 2
BF16_ROWS = 16

DA_HEADS = 4
DA_QK_DIM = 64
DA_V_DIM = 128
NSA_GROUPS = 2
NSA_REP = 4
NSA_HEAD_DIM = 64
CMP_LEN = 32
CMP_STRIDE = 16
SEL_LEN = 64
SEL_TOPK = 16
WIN = 512
FORCE_SCORE = 1e4
XA_HEADS = 4
SEL_MASK_BIG = 32768.0
LOG2E = 1.4426950408889634
QK_SCALE_LOG2 = 0.125 * LOG2E
POS_FEATS = 6

VMEM_LIMIT = 56 * 1024 * 1024

_NT = (((1,), (1,)), ((), ()))


def _rms(x, g):
    ms = jnp.mean(x * x, axis=-1, keepdims=True)
    return x * lax.rsqrt(ms + EPS) * g


def _dot(a, b):
    return jnp.dot(a, b, preferred_element_type=F32)


def _dot_nt(a, b):
    return lax.dot_general(a, b, _NT, preferred_element_type=F32)


def _const_spec(shape):
    nd = len(shape)
    return pl.BlockSpec(shape, lambda *_: (0,) * nd, pipeline_mode=pl.Buffered(1))


def _alibi_slopes(n):
    return [2.0 ** (-8.0 * (i + 1) / n) for i in range(n)]


def _softmax_step(s, m_old, l_old, acc_old, vt, ones):
    m_new = jnp.maximum(m_old, jnp.max(s, axis=0, keepdims=True))
    alpha = jnp.exp2(m_old - m_new)
    p = jnp.exp2(s - m_new).astype(BF)
    l_new = alpha * l_old + _dot(ones, p)[0:1]
    acc_new = alpha * acc_old + _dot(vt, p)
    return m_new, l_new, acc_new


def _ffn_kernel(x_ref, pre_ref, post_ref, wg_ref, wu_ref, wd_ref, o_ref):
    x = x_ref[...]
    h = _rms(x, pre_ref[...]).astype(BF)
    g = _dot(h, wg_ref[...])
    u = _dot(h, wu_ref[...])
    a = (g * jax.nn.sigmoid(g) * u).astype(BF)
    y = _dot(a, wd_ref[...])
    o_ref[...] = x + 0.5 * _rms(y, post_ref[...])


def _ffn(x2, pre_g, post_g, wg, wu, wd, *, tm):
    n, d = x2.shape
    f = wg.shape[1]
    return pl.pallas_call(
        _ffn_kernel,
        out_shape=jax.ShapeDtypeStruct((n, d), F32),
        grid=(n // tm,),
        in_specs=[
            pl.BlockSpec((tm, d), lambda i: (i, 0)),
            _const_spec((1, d)), _const_spec((1, d)),
            _const_spec((d, f)), _const_spec((d, f)), _const_spec((f, d)),
        ],
        out_specs=pl.BlockSpec((tm, d), lambda i: (i, 0)),
        compiler_params=pltpu.CompilerParams(
            dimension_semantics=("arbitrary",), vmem_limit_bytes=VMEM_LIMIT),
        name="ffn",
    )(x2, pre_g, post_g, wg, wu, wd)


def _mixin_kernel(x_ref, g_ref, w_ref, qfda_ref, qfn_ref,
                  daq, dak, davt, nq, nks, nkw, nvst, nvwt, nkc, nvc, gat, *, tk_da, tk_nsa):
    tm = x_ref.shape[0]
    t0 = pl.program_id(1) * tm
    h = _rms(x_ref[...], g_ref[...]).astype(BF)
    z = _dot(h, w_ref[...])

    lane = lax.broadcasted_iota(jnp.int32, (tm, LANES), 1)
    row = lax.broadcasted_iota(jnp.int32, (tm, LANES), 0) + t0
    pa = (row >> 6).astype(F32)
    pb = (row & 63).astype(F32)
    low = lane < HALF
    pab = jnp.where((lane & 1) == 0, pa, pb)
    pos_lo = jnp.where(lane < HALF, 0.0, jnp.where(lane < HALF + POS_FEATS, pab, 0.0))
    pos_hi = jnp.where(lane < POS_FEATS, pab, 0.0)
    blk_onehot = jnp.where(lane == (row >> 6), 1.0, 0.0)

    def ch(i):
        return z[:, i * LANES:(i + 1) * LANES]

    def put(ref, j, val):
        ref[:, j * LANES:(j + 1) * LANES] = val.astype(ref.dtype)

    for c in range(4):
        q = ch(c) * QK_SCALE_LOG2
        put(daq, 2 * c, jnp.where(low, q, qfda_ref[:, (2 * c) * LANES:(2 * c + 1) * LANES]))
        put(daq, 2 * c + 1, jnp.where(low, qfda_ref[:, (2 * c + 1) * LANES:(2 * c + 2) * LANES], q))
        k = ch(4 + c)
        put(dak, 2 * c, jnp.where(low, k, pos_lo))
        put(dak, 2 * c + 1, jnp.where(low, pos_hi, k))
        vt = ch(8 + c).T
        for s in range(tm // tk_da):
            davt[s, c * LANES:(c + 1) * LANES, :] = vt[:, s * tk_da:(s + 1) * tk_da].astype(BF)
        q = ch(12 + c) * QK_SCALE_LOG2
        put(nq, 2 * c, jnp.where(low, q, qfn_ref[:, (2 * c) * LANES:(2 * c + 1) * LANES]))
        put(nq, 2 * c + 1, jnp.where(low, qfn_ref[:, (2 * c + 1) * LANES:(2 * c + 2) * LANES], q))
    k = ch(16)
    put(nks, 0, jnp.where(low, k, pos_lo))
    put(nks, 1, blk_onehot)
    put(nks, 2, jnp.where(low, pos_hi, k))
    put(nks, 3, blk_onehot)
    k = ch(17)
    put(nkw, 0, jnp.where(low, k, pos_lo))
    put(nkw, 1, jnp.where(low, pos_hi, k))
    for src, ref in ((18, nvst), (19, nvwt)):
        vt = ch(src).T
        for s in range(tm // tk_nsa):
            ref[s] = vt[:, s * tk_nsa:(s + 1) * tk_nsa].astype(BF)
    nkc[...] = ch(20)
    nvc[...] = ch(21)
    gat[...] = jax.nn.sigmoid(ch(22))


def _mixin(x, g, w, qf_da, qf_n, *, tm, tk_da, tk_nsa):
    b, t, d = x.shape
    wcols = w.shape[1]
    assert tm % tk_da == 0 and tm % tk_nsa == 0
    row = lambda wd, dt: (jax.ShapeDtypeStruct((b, t, wd), dt),
                          pl.BlockSpec((None, tm, wd), lambda i, j: (i, j, 0)))
    tr = lambda rows, tk: (jax.ShapeDtypeStruct((b, t // tk, rows, tk), BF),
                           pl.BlockSpec((None, tm // tk, rows, tk), lambda i, j: (i, j, 0, 0)))
    outs = [row(1024, BF), row(1024, BF), tr(512, tk_da), row(1024, BF), row(512, BF), row(256, BF),
            tr(LANES, tk_nsa), tr(LANES, tk_nsa), row(LANES, F32), row(LANES, F32), row(LANES, F32)]
    return pl.pallas_call(
        functools.partial(_mixin_kernel, tk_da=tk_da, tk_nsa=tk_nsa),
        out_shape=[o[0] for o in outs],
        grid=(b, t // tm),
        in_specs=[
            pl.BlockSpec((None, tm, d), lambda i, j: (i, j, 0)),
            _const_spec((1, d)), _const_spec((d, wcols)),
            _const_spec((1, 1024)), _const_spec((1, 1024)),
        ],
        out_specs=[o[1] for o in outs],
        compiler_params=pltpu.CompilerParams(
            dimension_semantics=("arbitrary", "arbitrary"), vmem_limit_bytes=VMEM_LIMIT),
        name="mixin",
    )(x, g, w, qf_da, qf_n)


def _da_kernel(q_ref, k_ref, vt_ref, lq1, lk1, lq2, lk2, sg_ref, o_ref, s_ref, acc_ref, m_ref, l_ref,
               *, lam_init, tq, strip):
    qi = pl.program_id(2)
    m_ref[...] = jnp.full(m_ref.shape, NEG, F32)
    l_ref[...] = jnp.zeros(l_ref.shape, F32)
    acc_ref[...] = jnp.zeros(acc_ref.shape, F32)
    key = lax.broadcasted_iota(jnp.int32, (tq, strip), 0)
    qry = lax.broadcasted_iota(jnp.int32, (tq, strip), 1)
    ones = jnp.ones((BF16_ROWS, tq), BF)

    def scores(j, buf):
        ks = pl.multiple_of(j * tq, tq)
        for mp in range(2):
            s_ref[buf, mp] = _dot_nt(k_ref[pl.ds(ks, tq), mp * LANES:(mp + 1) * LANES],
                                     q_ref[:, mp * LANES:(mp + 1) * LANES])

    def consume(j, buf, masked):
        vt = vt_ref[j]
        for mp in range(2):
            for c in range(tq // strip):
                sl = slice(c * strip, (c + 1) * strip)
                s = s_ref[buf, mp, :, sl]
                if masked:
                    s = jnp.where(key <= qry + c * strip, s, NEG)
                m_ref[mp, :, sl], l_ref[mp, :, sl], acc_ref[mp, :, sl] = _softmax_step(
                    s, m_ref[mp, :, sl], l_ref[mp, :, sl], acc_ref[mp, :, sl], vt, ones)

    scores(0, 0)

    def pair(p, carry):
        scores(2 * p + 1, 1)
        consume(2 * p, 0, False)
        scores(2 * p + 2, 0)
        consume(2 * p + 1, 1, False)
        return carry

    lax.fori_loop(0, qi // 2, pair, 0)

    @pl.when(qi % 2 == 1)
    def _():
        scores(qi, 1)
        consume(qi - 1, 0, False)
        consume(qi, 1, True)

    @pl.when(qi % 2 == 0)
    def _():
        consume(qi, 0, True)

    lam = (jnp.exp(jnp.sum(lq1[...] * lk1[...], axis=1, keepdims=True))
           - jnp.exp(jnp.sum(lq2[...] * lk2[...], axis=1, keepdims=True)) + lam_init)
    o = (acc_ref[0] / l_ref[0] - lam * (acc_ref[1] / l_ref[1])).T
    o_ref[...] = (_rms(o, sg_ref[...]) * (1.0 - lam_init)).astype(o_ref.dtype)


def _da_attention(daq, dak, davt, lq1, lk1, lq2, lk2, subln_g, *, lam_init, tq):
    b, t, _ = daq.shape
    nk = t // tq
    vec = _const_spec((1, DA_QK_DIM))
    return pl.pallas_call(
        functools.partial(_da_kernel, lam_init=lam_init, tq=tq, strip=2 * LANES),
        out_shape=jax.ShapeDtypeStruct((b, t, DA_HEADS * DA_V_DIM), BF),
        grid=(b, DA_HEADS, t // tq),
        in_specs=[
            pl.BlockSpec((None, tq, 2 * LANES), lambda i, h, j: (i, j, h)),
            pl.BlockSpec((None, t, 2 * LANES), lambda i, h, j: (i, 0, h)),
            pl.BlockSpec((None, nk, DA_V_DIM, tq), lambda i, h, j: (i, 0, h, 0)),
            vec, vec, vec, vec, _const_spec((1, DA_V_DIM)),
        ],
        out_specs=pl.BlockSpec((None, tq, DA_V_DIM), lambda i, h, j: (i, j, h)),
        scratch_shapes=[
            pltpu.VMEM((2, 2, tq, tq), F32),
            pltpu.VMEM((2, DA_V_DIM, tq), F32),
            pltpu.VMEM((2, 1, tq), F32),
            pltpu.VMEM((2, 1, tq), F32),
        ],
        compiler_params=pltpu.CompilerParams(
            dimension_semantics=("arbitrary", "arbitrary", "arbitrary"), vmem_limit_bytes=VMEM_LIMIT),
        name="diff_attn",
    )(daq, dak, davt, lq1, lk1, lq2, lk2, subln_g)


def _compress_kernel(xk_ref, xv_ref, pek_ref, pev_ref, w1k_ref, w1v_ref, w2k_ref, w2v_ref, cpos_ref,
                     kc_ref, vct_ref):
    ncp = xk_ref.shape[0]
    for x_ref, pe_ref, w1_ref, w2_ref, is_v in ((xk_ref, pek_ref, w1k_ref, w2k_ref, False),
                                                (xv_ref, pev_ref, w1v_ref, w2v_ref, True)):
        x = x_ref[...]
        xa = (x + pe_ref[0]).astype(BF)
        xb = (x + pe_ref[1]).astype(BF)
        for g in range(NSA_GROUPS):
            first = _dot(xa, w1_ref[0, g])
            second = _dot(xb, w1_ref[1, g])
            hid = first + pltpu.roll(second, ncp - 1, 0)
            act = (hid * jax.nn.sigmoid(hid)).astype(BF)
            out = _dot(act, w2_ref[g])
            if is_v:
                vct_ref[g] = out.T.astype(BF)
            else:
                kc_ref[g] = (out + cpos_ref[g]).astype(BF)


def _compress(xk, xv, pek, pev, w1k, w1v, w2k, w2v, cpos):
    b, ncp, width = xk.shape
    x_spec = pl.BlockSpec((None, ncp, width), lambda i: (i, 0, 0))
    return pl.pallas_call(
        _compress_kernel,
        out_shape=[jax.ShapeDtypeStruct((b, NSA_GROUPS, ncp, LANES), BF),
                   jax.ShapeDtypeStruct((b, NSA_GROUPS, LANES, ncp), BF)],
        grid=(b,),
        in_specs=[x_spec, x_spec,
                  _const_spec(pek.shape), _const_spec(pev.shape),
                  _const_spec(w1k.shape), _const_spec(w1v.shape),
                  _const_spec(w2k.shape), _const_spec(w2v.shape), _const_spec(cpos.shape)],
        out_specs=[pl.BlockSpec((None, NSA_GROUPS, ncp, LANES), lambda i: (i, 0, 0, 0)),
                   pl.BlockSpec((None, NSA_GROUPS, LANES, ncp), lambda i: (i, 0, 0, 0))],
        compiler_params=pltpu.CompilerParams(
            dimension_semantics=("arbitrary",), vmem_limit_bytes=VMEM_LIMIT),
        name="compress",
    )(xk, xv, pek, pev, w1k, w1v, w2k, w2v, cpos)


def _cmp_kernel(nq_ref, kc_ref, vct_ref, mt_ref, o_ref, sf_ref, ob_ref, *, tq, topk):
    ncp = kc_ref.shape[1]
    ns = mt_ref.shape[0]
    t0 = pl.program_id(1) * tq
    cidx = lax.broadcasted_iota(jnp.int32, (ncp, tq), 0)
    tpos = lax.broadcasted_iota(jnp.int32, (ncp, tq), 1) + t0
    valid = (cidx * CMP_STRIDE + (CMP_LEN - 1)) <= tpos
    col_live = (lax.broadcasted_iota(jnp.int32, (1, tq), 1) + t0) >= CMP_LEN - 1
    lane = lax.broadcasted_iota(jnp.int32, (tq, LANES), 1)
    mt = mt_ref[...]

    blk = lax.broadcasted_iota(jnp.int32, (ns, tq), 0)
    cur = (lax.broadcasted_iota(jnp.int32, (ns, tq), 1) + t0) >> 6
    sub8 = lax.broadcasted_iota(jnp.int32, (8, tq), 0)

    for g in range(NSA_GROUPS):
        kc = kc_ref[g]
        vct = vct_ref[g]
        imp = jnp.zeros((ns, tq), F32)
        for r in range(NSA_REP):
            c = 2 * r + g
            q = nq_ref[:, c * LANES:(c + 1) * LANES]
            s = jnp.where(valid, _dot_nt(kc, q), NEG)
            m = jnp.max(s, axis=0, keepdims=True)
            e = jnp.exp2(s - m)
            p = e * jnp.where(col_live, 1.0 / jnp.sum(e, axis=0, keepdims=True), 0.0)
            p_hi = p.astype(BF)
            p_lo = (p - p_hi.astype(F32)).astype(BF)
            imp = imp + _dot(mt, p_hi) + _dot(mt, p_lo)
            o = _dot(vct, p_hi).T
            if g == 0:
                ob_ref[r] = o
            else:
                o_ref[:, r * LANES:(r + 1) * LANES] = jnp.where(lane < HALF, ob_ref[r], o).astype(o_ref.dtype)

        score = jnp.where(blk == 0, FORCE_SCORE,
                          jnp.where(blk == cur, FORCE_SCORE,
                                    jnp.where(blk == cur - 1, FORCE_SCORE,
                                              jnp.where(blk <= cur, imp, -1.0))))
        rows = [jnp.broadcast_to(score[i:i + 1, :], (8, tq)) for i in range(ns)]
        feats = []
        for kt in range(ns // 8):
            sc = score[8 * kt:8 * kt + 8, :]
            cnt = jnp.zeros((8, tq), F32)
            for i in range(ns):
                if i < 8 * kt:
                    beat = jnp.where(rows[i] >= sc, 1.0, 0.0)
                elif i >= 8 * kt + 8:
                    beat = jnp.where(rows[i] > sc, 1.0, 0.0)
                else:
                    beat = jnp.where(sub8 + 8 * kt > i,
                                     jnp.where(rows[i] >= sc, 1.0, 0.0),
                                     jnp.where(rows[i] > sc, 1.0, 0.0))
                cnt = cnt + beat
            feats.append(jnp.where(cnt < topk, 0.0, -SEL_MASK_BIG))
        feats.append(jnp.zeros((LANES - ns, tq), F32))
        sf_ref[g] = jnp.concatenate(feats, axis=0).T.astype(BF)


def _cmp_topk(nq, kc, vct, mt, *, tq, topk):
    b, t, _ = nq.shape
    ncp = kc.shape[2]
    return pl.pallas_call(
        functools.partial(_cmp_kernel, tq=tq, topk=topk),
        out_shape=[jax.ShapeDtypeStruct((b, t, NSA_REP * LANES), BF),
                   jax.ShapeDtypeStruct((b, NSA_GROUPS, t, LANES), BF)],
        grid=(b, t // tq),
        in_specs=[
            pl.BlockSpec((None, tq, 2 * NSA_REP * LANES), lambda i, j: (i, j, 0)),
            pl.BlockSpec((None, NSA_GROUPS, ncp, LANES), lambda i, j: (i, 0, 0, 0)),
            pl.BlockSpec((None, NSA_GROUPS, LANES, ncp), lambda i, j: (i, 0, 0, 0)),
            _const_spec(mt.shape),
        ],
        out_specs=[pl.BlockSpec((None, tq, NSA_REP * LANES), lambda i, j: (i, j, 0)),
                   pl.BlockSpec((None, NSA_GROUPS, tq, LANES), lambda i, j: (i, 0, j, 0))],
        scratch_shapes=[pltpu.VMEM((NSA_REP, tq, LANES), F32)],
        compiler_params=pltpu.CompilerParams(
            dimension_semantics=("arbitrary", "arbitrary"), vmem_limit_bytes=VMEM_LIMIT),
        name="cmp_topk",
    )(nq, kc, vct, mt)


def _nsa_flash_kernel(*refs, tq, window, selected):
    if selected:
        nq_ref, sf_ref, k_ref, vt_ref, o_ref, qs_ref, s_ref, acc_ref, m_ref, l_ref, ob_ref = refs
    else:
        nq_ref, k_ref, vt_ref, o_ref, qs_ref, s_ref, acc_ref, m_ref, l_ref, ob_ref = refs
    qi = pl.program_id(1)
    kd = qs_ref.shape[2]
    cols = NSA_REP * tq
    key = lax.broadcasted_iota(jnp.int32, (tq, tq), 0)
    qry = lax.broadcasted_iota(jnp.int32, (tq, tq), 1)
    ones = jnp.ones((BF16_ROWS, tq), BF)

    for g in range(NSA_GROUPS):
        for r in range(NSA_REP):
            c = 2 * r + g
            qs_ref[g, r * tq:(r + 1) * tq, 0:LANES] = nq_ref[:, c * LANES:(c + 1) * LANES]
            if selected:
                qs_ref[g, r * tq:(r + 1) * tq, LANES:2 * LANES] = sf_ref[g]

    m_ref[...] = jnp.full(m_ref.shape, NEG, F32)
    l_ref[...] = jnp.zeros(l_ref.shape, F32)
    acc_ref[...] = jnp.zeros(acc_ref.shape, F32)

    def scores(j, buf):
        ks = pl.multiple_of(j * tq, tq)
        for g in range(NSA_GROUPS):
            s_ref[buf, g] = _dot_nt(k_ref[pl.ds(ks, tq), g * kd:(g + 1) * kd], qs_ref[g])

    def consume(j, buf, mode, live=None):
        if mode == "edge":
            edge_qry = jnp.where(live, qry, tq)
        for g in range(NSA_GROUPS):
            vt = vt_ref[j, g * NSA_HEAD_DIM:(g + 1) * NSA_HEAD_DIM, :]
            for r in range(NSA_REP):
                sl = slice(r * tq, (r + 1) * tq)
                s = s_ref[buf, g, :, sl]
                if mode == "diag":
                    s = jnp.where(key <= qry, s, NEG)
                elif mode == "edge":
                    s = jnp.where(key > edge_qry, s, NEG)
                elif live is not None:
                    s = jnp.where(live, s, NEG)
                m_ref[g, :, sl], l_ref[g, :, sl], acc_ref[g, :, sl] = _softmax_step(
                    s, m_ref[g, :, sl], l_ref[g, :, sl], acc_ref[g, :, sl], vt, ones)

    if window:
        nback = WIN // tq
        for i in range(nback + 1):
            j = qi - nback + i
            scores(jnp.maximum(j, 0), i)
            if i == nback:
                consume(j, i, "diag")
            else:
                consume(jnp.maximum(j, 0), i, "edge" if i == 0 else "full", j >= 0)
    else:
        scores(0, 0)

        def pair(p, carry):
            scores(2 * p + 1, 1)
            consume(2 * p, 0, "full")
            scores(2 * p + 2, 0)
            consume(2 * p + 1, 1, "full")
            return carry

        lax.fori_loop(0, qi // 2, pair, 0)

        @pl.when(qi % 2 == 1)
        def _():
            scores(qi, 1)
            consume(qi - 1, 0, "full")
            consume(qi, 1, "diag")

        @pl.when(qi % 2 == 0)
        def _():
            consume(qi, 0, "diag")

    for g in range(NSA_GROUPS):
        o = acc_ref[g] / l_ref[g]
        for r in range(NSA_REP):
            ob_ref[r, g * NSA_HEAD_DIM:(g + 1) * NSA_HEAD_DIM, :] = o[:, r * tq:(r + 1) * tq]

    for r in range(NSA_REP):
        o_ref[:, r * LANES:(r + 1) * LANES] = ob_ref[r].T.astype(o_ref.dtype)


def _nsa_flash(nq, k, vt, sf=None, *, tq, window):
    b, t, _ = nq.shape
    selected = sf is not None
    kd = 2 * LANES if selected else LANES
    nk = t // tq
    assert vt.shape == (b, nk, LANES, tq) and k.shape == (b, t, NSA_GROUPS * kd)
    in_specs = [pl.BlockSpec((None, tq, 2 * NSA_REP * LANES), lambda i, j: (i, j, 0))]
    args = [nq]
    if selected:
        in_specs.append(pl.BlockSpec((None, NSA_GROUPS, tq, LANES), lambda i, j: (i, 0, j, 0)))
        args.append(sf)
    in_specs.append(pl.BlockSpec((None, t, NSA_GROUPS * kd), lambda i, j: (i, 0, 0)))
    args.append(k)
    in_specs.append(pl.BlockSpec((None, nk, LANES, tq), lambda i, j: (i, 0, 0, 0)))
    args.append(vt)
    cols = NSA_REP * tq
    return pl.pallas_call(
        functools.partial(_nsa_flash_kernel, tq=tq, window=window, selected=selected),
        out_shape=jax.ShapeDtypeStruct((b, t, NSA_REP * LANES), BF),
        grid=(b, t // tq),
        in_specs=in_specs,
        out_specs=pl.BlockSpec((None, tq, NSA_REP * LANES), lambda i, j: (i, j, 0)),
        scratch_shapes=[
            pltpu.VMEM((NSA_GROUPS, cols, kd), BF),
            pltpu.VMEM((WIN // tq + 1 if window else 2, NSA_GROUPS, tq, cols), F32),
            pltpu.VMEM((NSA_GROUPS, NSA_HEAD_DIM, cols), F32),
            pltpu.VMEM((NSA_GROUPS, 1, cols), F32),
            pltpu.VMEM((NSA_GROUPS, 1, cols), F32),
            pltpu.VMEM((NSA_REP, LANES, tq), F32),
        ],
        compiler_params=pltpu.CompilerParams(
            dimension_semantics=("arbitrary", "arbitrary"), vmem_limit_bytes=VMEM_LIMIT),
        name="nsa_window" if window else "nsa_selected",
    )(*args)


def _mixout_kernel(x_ref, oa_ref, oc_ref, os_ref, ow_ref, gat_ref, e_ref, wa_ref, wb_ref, post_ref, o_ref):
    gt = gat_ref[...]
    g_hi = gt.astype(BF)
    g_lo = (gt - g_hi.astype(F32)).astype(BF)
    g_split = jnp.concatenate([g_hi, g_lo], axis=1)
    ob = None
    for c, ref in enumerate((oc_ref, os_ref, ow_ref)):
        gate = _dot(g_split, e_ref[c])
        term = gate * ref[...].astype(F32)
        ob = term if ob is None else ob + term
    y = _dot(oa_ref[...], wa_ref[...]) + _dot(ob.astype(BF), wb_ref[...])
    o_ref[...] = x_ref[...] + _rms(y, post_ref[...])


def _mixout(x2, oa, oc, osel, ow, gates, e, wa, wb, post_g, *, tm):
    n, d = x2.shape
    row = lambda w: pl.BlockSpec((tm, w), lambda i: (i, 0))
    return pl.pallas_call(
        _mixout_kernel,
        out_shape=jax.ShapeDtypeStruct((n, d), F32),
        grid=(n // tm,),
        in_specs=[row(d), row(512), row(512), row(512), row(512), row(LANES),
                  _const_spec(e.shape), _const_spec(wa.shape), _const_spec(wb.shape), _const_spec((1, d))],
        out_specs=row(d),
        compiler_params=pltpu.CompilerParams(
            dimension_semantics=("arbitrary",), vmem_limit_bytes=VMEM_LIMIT),
        name="mixout",
    )(x2, oa, oc, osel, ow, gates, e, wa, wb, post_g)


def _memkv_kernel(m_ref, g_ref, wk_ref, wv_ref, k_ref, v_ref):
    m = _rms(m_ref[...], g_ref[...]).astype(BF)
    k_ref[...] = _dot(m, wk_ref[...]).astype(BF)
    v_ref[...] = _dot(m, wv_ref[...]).astype(BF)


def _memkv(mem, g, wk, wv):
    b, nm, d = mem.shape
    spec = pl.BlockSpec((None, nm, d), lambda i: (i, 0, 0))
    return pl.pallas_call(
        _memkv_kernel,
        out_shape=[jax.ShapeDtypeStruct((b, nm, d), BF)] * 2,
        grid=(b,),
        in_specs=[spec, _const_spec((1, d)), _const_spec(wk.shape), _const_spec(wv.shape)],
        out_specs=[spec, spec],
        compiler_params=pltpu.CompilerParams(
            dimension_semantics=("arbitrary",), vmem_limit_bytes=VMEM_LIMIT),
        name="mem_kv",
    )(mem, g, wk, wv)


def _xa_kernel(x_ref, pre_ref, post_ref, wq_ref, wo_ref, k_ref, v_ref, o_ref, *, scale):
    x = x_ref[...]
    h = _rms(x, pre_ref[...]).astype(BF)
    q = (_dot(h, wq_ref[...]) * scale).astype(BF)
    hd = q.shape[1] // XA_HEADS
    outs = []
    for i in range(XA_HEADS):
        sl = slice(i * hd, (i + 1) * hd)
        s = _dot_nt(q[:, sl], k_ref[:, sl])
        m = jnp.max(s, axis=1, keepdims=True)
        e = jnp.exp(s - m)
        p = e / jnp.sum(e, axis=1, keepdims=True)
        outs.append(_dot(p.astype(BF), v_ref[:, sl]).astype(BF))
    y = _dot(jnp.concatenate(outs, axis=1), wo_ref[...])
    o_ref[...] = x + _rms(y, post_ref[...])


def _xa(x, pre_g, post_g, wq, wo, k, v, *, tm):
    b, t, d = x.shape
    nm = k.shape[1]
    hd = wq.shape[1] // XA_HEADS
    row = pl.BlockSpec((None, tm, d), lambda i, j: (i, j, 0))
    kv = pl.BlockSpec((None, nm, k.shape[2]), lambda i, j: (i, 0, 0))
    return pl.pallas_call(
        functools.partial(_xa_kernel, scale=float(hd) ** -0.5),
        out_shape=jax.ShapeDtypeStruct((b, t, d), F32),
        grid=(b, t // tm),
        in_specs=[row, _const_spec((1, d)), _const_spec((1, d)),
                  _const_spec(wq.shape), _const_spec(wo.shape), kv, kv],
        out_specs=row,
        compiler_params=pltpu.CompilerParams(
            dimension_semantics=("arbitrary", "arbitrary"), vmem_limit_bytes=VMEM_LIMIT),
        name="mem_xattn",
    )(x, pre_g, post_g, wq, wo, k, v)


def _bf16_terms(x, n):
    terms = []
    for _ in range(n):
        u = np.float32(x).view(np.uint32)
        hi = np.uint32((int(u) + 0x7FFF + ((int(u) >> 16) & 1)) & 0xFFFF0000).view(np.float32)
        terms.append(float(hi))
        x = float(np.float64(x) - np.float64(hi))
    return terms


def _query_feature_rows(slopes_by_chunk):
    out = np.zeros((1, len(slopes_by_chunk) * LANES), np.float32)
    for j, s in enumerate(slopes_by_chunk):
        base = j * LANES + (HALF if j % 2 == 0 else 0)
        c64 = _bf16_terms(s * 64.0 * LOG2E, POS_FEATS // 2)
        c1 = _bf16_terms(s * LOG2E, POS_FEATS // 2)
        for i in range(POS_FEATS // 2):
            out[0, base + 2 * i] = c64[i]
            out[0, base + 2 * i + 1] = c1[i]
    return jnp.asarray(out)


def _static_tables(t):
    ncp = t // CMP_STRIDE
    ns = t // SEL_LEN
    da = _alibi_slopes(DA_HEADS)
    qf_da = _query_feature_rows([da[j // 2] for j in range(2 * DA_HEADS)])
    nsa = _alibi_slopes(NSA_GROUPS * NSA_REP)
    qf_n = _query_feature_rows([nsa[(j % 2) * NSA_REP + j // 2] for j in range(2 * NSA_REP)])

    c_end = np.arange(ncp) * CMP_STRIDE + CMP_LEN - 1
    cpos = np.zeros((NSA_GROUPS, ncp, LANES), np.float32)
    for i in range(POS_FEATS // 2):
        cpos[0, :, HALF + 2 * i] = c_end // 64
        cpos[0, :, HALF + 2 * i + 1] = c_end % 64
        cpos[1, :, 2 * i] = c_end // 64
        cpos[1, :, 2 * i + 1] = c_end % 64

    c_start = np.arange(ncp) * CMP_STRIDE
    s_start = np.arange(ns) * SEL_LEN
    overlap = np.clip(np.minimum(c_start[:, None] + CMP_LEN, s_start[None, :] + SEL_LEN)
                      - np.maximum(c_start[:, None], s_start[None, :]), 0, None)
    mt = (overlap.astype(np.float32) / CMP_LEN).T
    mt[:, ncp - 1] = 0.0

    e = np.zeros((3, LANES, NSA_REP * LANES), np.float32)
    for g in range(NSA_GROUPS):
        for r in range(NSA_REP):
            for c in range(3):
                col = r * LANES + g * HALF
                e[c, g * NSA_REP * 3 + r * 3 + c, col:col + HALF] = 1.0
    e = np.concatenate([e, e], axis=1)
    return qf_da, qf_n, jnp.asarray(cpos), jnp.asarray(mt, BF), jnp.asarray(e, BF)


def _compress_weights(pe, w1, w2):
    hidden = w1.shape[1]
    w1r = w1.reshape(2, CMP_STRIDE, NSA_HEAD_DIM, hidden)
    w1g = jnp.zeros((2, NSA_GROUPS, CMP_STRIDE, NSA_GROUPS, NSA_HEAD_DIM, hidden), w1.dtype)
    for g in range(NSA_GROUPS):
        w1g = w1g.at[:, g, :, g].set(w1r)
    w1g = w1g.reshape(2, NSA_GROUPS, CMP_STRIDE * LANES, hidden).astype(BF)
    per = pe.reshape(2, CMP_STRIDE, 1, NSA_HEAD_DIM)
    peg = jnp.broadcast_to(per, (2, CMP_STRIDE, NSA_GROUPS, NSA_HEAD_DIM)).reshape(2, 1, CMP_STRIDE * LANES)
    w2g = jnp.zeros((NSA_GROUPS, hidden, NSA_GROUPS, NSA_HEAD_DIM), w2.dtype)
    for g in range(NSA_GROUPS):
        w2g = w2g.at[g, :, g].set(w2)
    w2g = w2g.reshape(NSA_GROUPS, hidden, LANES).astype(BF)
    return peg, w1g, w2g


def kernel(x, mem, ffn1_pre_g, ffn1_post_g, ffn1_w_gate, ffn1_w_up, ffn1_w_down, mix_pre_g, mix_post_g,
           w_mix_in, da_lambda_q1, da_lambda_k1, da_lambda_q2, da_lambda_k2, da_subln_g, cmp_k_pe,
           cmp_k_w1, cmp_k_w2, cmp_v_pe, cmp_v_w1, cmp_v_w2, w_mix_out, xa_pre_g, xa_post_g, mem_norm_g,
           xa_w_q, xa_w_k, xa_w_v, xa_w_o, ffn2_pre_g, ffn2_post_g, ffn2_w_gate, ffn2_w_up, ffn2_w_down):
    b, t, d = x.shape
    depth = ffn1_pre_g.shape[0]
    n = b * t
    tm = 512
    tq_da = 512
    tq = 256
    ns = t // SEL_LEN
    assert t % tm == 0 and ns <= LANES and WIN % tq == 0
    assert d == DA_HEADS * DA_V_DIM + NSA_GROUPS * NSA_REP * NSA_HEAD_DIM
    qf_da, qf_n, cpos, mt, e = _static_tables(t)
    row = lambda v: v.reshape(1, -1)

    for l in range(depth):
        x2 = _ffn(x.reshape(n, d), row(ffn1_pre_g[l]), row(ffn1_post_g[l]),
                  ffn1_w_gate[l].astype(BF), ffn1_w_up[l].astype(BF), ffn1_w_down[l].astype(BF), tm=tm)

        w = w_mix_in[l]
        dq = DA_HEADS * 2 * DA_QK_DIM
        dv = DA_HEADS * DA_V_DIM
        nqw = NSA_GROUPS * NSA_REP * NSA_HEAD_DIM
        nkv = NSA_GROUPS * NSA_HEAD_DIM
        o0 = 2 * dq + dv
        w_nq = w[:, o0:o0 + nqw].reshape(d, NSA_GROUPS, NSA_REP, NSA_HEAD_DIM).transpose(0, 2, 1, 3).reshape(d, nqw)
        o1 = o0 + nqw
        seg = lambda i: w[:, o1 + i * nkv:o1 + (i + 1) * nkv]
        w_g = w[:, o1 + 6 * nkv:]
        w_g = jnp.pad(w_g, ((0, 0), (0, LANES - w_g.shape[1])))
        w_big = jnp.concatenate([w[:, :o0], w_nq, seg(2), seg(4), seg(3), seg(5), seg(0), seg(1), w_g],
                                axis=1).astype(BF)
        (daq, dak, davt, nq, nks, nkw, nvst, nvwt, nkc, nvc, gates) = _mixin(
            x2.reshape(b, t, d), row(mix_pre_g[l]), w_big, qf_da, qf_n, tm=tm, tk_da=tq_da, tk_nsa=tq)

        lam_init = 0.8 - 0.6 * float(np.exp(-0.3 * l))
        o_a = _da_attention(daq, dak, davt, row(da_lambda_q1[l]), row(da_lambda_k1[l]),
                            row(da_lambda_q2[l]), row(da_lambda_k2[l]), row(da_subln_g[l]),
                            lam_init=lam_init, tq=tq_da)

        pek, w1k, w2k = _compress_weights(cmp_k_pe[l], cmp_k_w1[l], cmp_k_w2[l])
        pev, w1v, w2v = _compress_weights(cmp_v_pe[l], cmp_v_w1[l], cmp_v_w2[l])
        ncp = t // CMP_STRIDE
        kc, vct = _compress(nkc.reshape(b, ncp, CMP_STRIDE * LANES), nvc.reshape(b, ncp, CMP_STRIDE * LANES),
                            pek, pev, w1k, w1v, w2k, w2v, cpos)
        o_cmp, sf = _cmp_topk(nq, kc, vct, mt, tq=tq, topk=min(SEL_TOPK, ns))
        o_sel = _nsa_flash(nq, nks, nvst, sf, tq=tq, window=False)
        o_win = _nsa_flash(nq, nkw, nvwt, tq=tq, window=True)

        wo = w_mix_out[l]
        wa = wo[:dv].astype(BF)
        wb = wo[dv:].reshape(NSA_GROUPS, NSA_REP, NSA_HEAD_DIM, d).transpose(1, 0, 2, 3).reshape(nqw, d).astype(BF)
        x3 = _mixout(x2, o_a.reshape(n, dv), o_cmp.reshape(n, nqw), o_sel.reshape(n, nqw),
                     o_win.reshape(n, nqw), gates.reshape(n, LANES), e, wa, wb, row(mix_post_g[l]), tm=tm)

        mk, mv = _memkv(mem, row(mem_norm_g[l]), xa_w_k[l].astype(BF), xa_w_v[l].astype(BF))
        x4 = _xa(x3.reshape(b, t, d), row(xa_pre_g[l]), row(xa_post_g[l]),
                 xa_w_q[l].astype(BF), xa_w_o[l].astype(BF), mk, mv, tm=tm)

        x = _ffn(x4.reshape(n, d), row(ffn2_pre_g[l]), row(ffn2_post_g[l]),
                 ffn2_w_gate[l].astype(BF), ffn2_w_up[l].astype(BF), ffn2_w_down[l].astype(BF),
                 tm=tm).reshape(b, t, d)
    return x
```

```python
import functools

import numpy as np
import jax
import jax.numpy as jnp
from jax import lax
from jax.experimental import pallas as pl
from jax.experimental.pallas import tpu as pltpu

BF = jnp.bfloat16
F32 = jnp.float32

EPS = 1e-6
NEG = -1e30
LANES = 128
HALF = LANES // 2
BF16_ROWS = 16

DA_HEADS = 4
DA_QK_DIM = 64
DA_V_DIM = 128
NSA_GROUPS = 2
NSA_REP = 4
NSA_HEAD_DIM = 64
CMP_LEN = 32
CMP_STRIDE = 16
SEL_LEN = 64
SEL_TOPK = 16
WIN = 512
FORCE_SCORE = 1e4
XA_HEADS = 4
SEL_MASK_BIG = 32768.0
LOG2E = 1.4426950408889634
QK_SCALE_LOG2 = 0.125 * LOG2E
POS_FEATS = 6

VMEM_LIMIT = 56 * 1024 * 1024

_NT = (((1,), (1,)), ((), ()))


def _rms(x, g):
    ms = jnp.mean(x * x, axis=-1, keepdims=True)
    return x * lax.rsqrt(ms + EPS) * g


def _dot(a, b):
    return jnp.dot(a, b, preferred_element_type=F32)


def _dot_nt(a, b):
    return lax.dot_general(a, b, _NT, preferred_element_type=F32)


def _const_spec(shape):
    nd = len(shape)
    return pl.BlockSpec(shape, lambda *_: (0,) * nd, pipeline_mode=pl.Buffered(1))


def _alibi_slopes(n):
    return [2.0 ** (-8.0 * (i + 1) / n) for i in range(n)]


def _softmax_step(s, m_old, l_old, acc_old, vt, ones):
    m_new = jnp.maximum(m_old, jnp.max(s, axis=0, keepdims=True))
    alpha = jnp.exp2(m_old - m_new)
    p = jnp.exp2(s - m_new).astype(BF)
    l_new = alpha * l_old + _dot(ones, p)[0:1]
    acc_new = alpha * acc_old + _dot(vt, p)
    return m_new, l_new, acc_new


def _ffn_kernel(x_ref, pre_ref, post_ref, wg_ref, wu_ref, wd_ref, o_ref):
    x = x_ref[...]
    h = _rms(x, pre_ref[...]).astype(BF)
    g = _dot(h, wg_ref[...])
    u = _dot(h, wu_ref[...])
    a = (g * jax.nn.sigmoid(g) * u).astype(BF)
    y = _dot(a, wd_ref[...])
    o_ref[...] = x + 0.5 * _rms(y, post_ref[...])


def _ffn(x2, pre_g, post_g, wg, wu, wd, *, tm):
    n, d = x2.shape
    f = wg.shape[1]
    return pl.pallas_call(
        _ffn_kernel,
        out_shape=jax.ShapeDtypeStruct((n, d), F32),
        grid=(n // tm,),
        in_specs=[
            pl.BlockSpec((tm, d), lambda i: (i, 0)),
            _const_spec((1, d)), _const_spec((1, d)),
            _const_spec((d, f)), _const_spec((d, f)), _const_spec((f, d)),
        ],
        out_specs=pl.BlockSpec((tm, d), lambda i: (i, 0)),
        compiler_params=pltpu.CompilerParams(
            dimension_semantics=("arbitrary",), vmem_limit_bytes=VMEM_LIMIT),
        name="ffn",
    )(x2, pre_g, post_g, wg, wu, wd)


def _mixin_kernel(x_ref, g_ref, w_ref, qfda_ref, qfn_ref,
                  daq, dak, davt, nq, nks, nkw, nvst, nvwt, nkc, nvc, gat, *, tk_da, tk_nsa):
    tm = x_ref.shape[0]
    t0 = pl.program_id(1) * tm
    h = _rms(x_ref[...], g_ref[...]).astype(BF)
    z = _dot(h, w_ref[...])

    lane = lax.broadcasted_iota(jnp.int32, (tm, LANES), 1)
    row = lax.broadcasted_iota(jnp.int32, (tm, LANES), 0) + t0
    pa = (row >> 6).astype(F32)
    pb = (row & 63).astype(F32)
    low = lane < HALF
    pab = jnp.where((lane & 1) == 0, pa, pb)
    pos_lo = jnp.where(lane < HALF, 0.0, jnp.where(lane < HALF + POS_FEATS, pab, 0.0))
    pos_hi = jnp.where(lane < POS_FEATS, pab, 0.0)
    blk_onehot = jnp.where(lane == (row >> 6), 1.0, 0.0)

    def ch(i):
        return z[:, i * LANES:(i + 1) * LANES]

    def put(ref, j, val):
        ref[:, j * LANES:(j + 1) * LANES] = val.astype(ref.dtype)

    for c in range(4):
        q = ch(c) * QK_SCALE_LOG2
        put(daq, 2 * c, jnp.where(low, q, qfda_ref[:, (2 * c) * LANES:(2 * c + 1) * LANES]))
        put(daq, 2 * c + 1, jnp.where(low, qfda_ref[:, (2 * c + 1) * LANES:(2 * c + 2) * LANES], q))
        k = ch(4 + c)
        put(dak, 2 * c, jnp.where(low, k, pos_lo))
        put(dak, 2 * c + 1, jnp.where(low, pos_hi, k))
        vt = ch(8 + c).T
        for s in range(tm // tk_da):
            davt[s, c * LANES:(c + 1) * LANES, :] = vt[:, s * tk_da:(s + 1) * tk_da].astype(BF)
        q = ch(12 + c) * QK_SCALE_LOG2
        put(nq, 2 * c, jnp.where(low, q, qfn_ref[:, (2 * c) * LANES:(2 * c + 1) * LANES]))
        put(nq, 2 * c + 1, jnp.where(low, qfn_ref[:, (2 * c + 1) * LANES:(2 * c + 2) * LANES], q))
    k = ch(16)
    put(nks, 0, jnp.where(low, k, pos_lo))
    put(nks, 1, blk_onehot)
    put(nks, 2, jnp.where(low, pos_hi, k))
    put(nks, 3, blk_onehot)
    k = ch(17)
    put(nkw, 0, jnp.where(low, k, pos_lo))
    put(nkw, 1, jnp.where(low, pos_hi, k))
    for src, ref in ((18, nvst), (19, nvwt)):
        vt = ch(src).T
        for s in range(tm // tk_nsa):
            ref[s] = vt[:, s * tk_nsa:(s + 1) * tk_nsa].astype(BF)
    nkc[...] = ch(20)
    nvc[...] = ch(21)
    gat[...] = jax.nn.sigmoid(ch(22))


def _mixin(x, g, w, qf_da, qf_n, *, tm, tk_da, tk_nsa):
    b, t, d = x.shape
    wcols = w.shape[1]
    assert tm % tk_da == 0 and tm % tk_nsa == 0
    row = lambda wd, dt: (jax.ShapeDtypeStruct((b, t, wd), dt),
                          pl.BlockSpec((None, tm, wd), lambda i, j: (i, j, 0)))
    tr = lambda rows, tk: (jax.ShapeDtypeStruct((b, t // tk, rows, tk), BF),
                           pl.BlockSpec((None, tm // tk, rows, tk), lambda i, j: (i, j, 0, 0)))
    outs = [row(1024, BF), row(1024, BF), tr(512, tk_da), row(1024, BF), row(512, BF), row(256, BF),
            tr(LANES, tk_nsa), tr(LANES, tk_nsa), row(LANES, F32), row(LANES, F32), row(LANES, F32)]
    return pl.pallas_call(
        functools.partial(_mixin_kernel, tk_da=tk_da, tk_nsa=tk_nsa),
        out_shape=[o[0] for o in outs],
        grid=(b, t // tm),
        in_specs=[
            pl.BlockSpec((None, tm, d), lambda i, j: (i, j, 0)),
            _const_spec((1, d)), _const_spec((d, wcols)),
            _const_spec((1, 1024)), _const_spec((1, 1024)),
        ],
        out_specs=[o[1] for o in outs],
        compiler_params=pltpu.CompilerParams(
            dimension_semantics=("arbitrary", "arbitrary"), vmem_limit_bytes=VMEM_LIMIT),
        name="mixin",
    )(x, g, w, qf_da, qf_n)


def _da_kernel(q_ref, k_ref, vt_ref, lq1, lk1, lq2, lk2, sg_ref, o_ref, s_ref, acc_ref, m_ref, l_ref,
               *, lam_init, tq, strip):
    qi = pl.program_id(2)
    m_ref[...] = jnp.full(m_ref.shape, NEG, F32)
    l_ref[...] = jnp.zeros(l_ref.shape, F32)
    acc_ref[...] = jnp.zeros(acc_ref.shape, F32)
    key = lax.broadcasted_iota(jnp.int32, (tq, strip), 0)
    qry = lax.broadcasted_iota(jnp.int32, (tq, strip), 1)
    ones = jnp.ones((BF16_ROWS, tq), BF)

    def scores(j, buf):
        ks = pl.multiple_of(j * tq, tq)
        for mp in range(2):
            s_ref[buf, mp] = _dot_nt(k_ref[pl.ds(ks, tq), mp * LANES:(mp + 1) * LANES],
                                     q_ref[:, mp * LANES:(mp + 1) * LANES])

    def consume(j, buf, masked):
        vt = vt_ref[j]
        for mp in range(2):
            for c in range(tq // strip):
                sl = slice(c * strip, (c + 1) * strip)
                s = s_ref[buf, mp, :, sl]
                if masked:
                    s = jnp.where(key <= qry + c * strip, s, NEG)
                m_ref[mp, :, sl], l_ref[mp, :, sl], acc_ref[mp, :, sl] = _softmax_step(
                    s, m_ref[mp, :, sl], l_ref[mp, :, sl], acc_ref[mp, :, sl], vt, ones)

    scores(0, 0)

    def pair(p, carry):
        scores(2 * p + 1, 1)
        consume(2 * p, 0, False)
        scores(2 * p + 2, 0)
        consume(2 * p + 1, 1, False)
        return carry

    lax.fori_loop(0, qi // 2, pair, 0)

    @pl.when(qi % 2 == 1)
    def _():
        scores(qi, 1)
        consume(qi - 1, 0, False)
        consume(qi, 1, True)

    @pl.when(qi % 2 == 0)
    def _():
        consume(qi, 0, True)

    lam = (jnp.exp(jnp.sum(lq1[...] * lk1[...], axis=1, keepdims=True))
           - jnp.exp(jnp.sum(lq2[...] * lk2[...], axis=1, keepdims=True)) + lam_init)
    o = (acc_ref[0] / l_ref[0] - lam * (acc_ref[1] / l_ref[1])).T
    o_ref[...] = (_rms(o, sg_ref[...]) * (1.0 - lam_init)).astype(o_ref.dtype)


def _da_attention(daq, dak, davt, lq1, lk1, lq2, lk2, subln_g, *, lam_init, tq):
    b, t, _ = daq.shape
    nk = t // tq
    vec = _const_spec((1, DA_QK_DIM))
    return pl.pallas_call(
        functools.partial(_da_kernel, lam_init=lam_init, tq=tq, strip=2 * LANES),
        out_shape=jax.ShapeDtypeStruct((b, t, DA_HEADS * DA_V_DIM), BF),
        grid=(b, DA_HEADS, t // tq),
        in_specs=[
            pl.BlockSpec((None, tq, 2 * LANES), lambda i, h, j: (i, j, h)),
            pl.BlockSpec((None, t, 2 * LANES), lambda i, h, j: (i, 0, h)),
            pl.BlockSpec((None, nk, DA_V_DIM, tq), lambda i, h, j: (i, 0, h, 0)),
            vec, vec, vec, vec, _const_spec((1, DA_V_DIM)),
        ],
        out_specs=pl.BlockSpec((None, tq, DA_V_DIM), lambda i, h, j: (i, j, h)),
        scratch_shapes=[
            pltpu.VMEM((2, 2, tq, tq), F32),
            pltpu.VMEM((2, DA_V_DIM, tq), F32),
            pltpu.VMEM((2, 1, tq), F32),
            pltpu.VMEM((2, 1, tq), F32),
        ],
        compiler_params=pltpu.CompilerParams(
            dimension_semantics=("arbitrary", "arbitrary", "arbitrary"), vmem_limit_bytes=VMEM_LIMIT),
        name="diff_attn",
    )(daq, dak, davt, lq1, lk1, lq2, lk2, subln_g)


def _compress_kernel(xk_ref, xv_ref, pek_ref, pev_ref, w1k_ref, w1v_ref, w2k_ref, w2v_ref, cpos_ref,
                     kc_ref, vct_ref):
    ncp = xk_ref.shape[0]
    for x_ref, pe_ref, w1_ref, w2_ref, is_v in ((xk_ref, pek_ref, w1k_ref, w2k_ref, False),
                                                (xv_ref, pev_ref, w1v_ref, w2v_ref, True)):
        x = x_ref[...]
        xa = (x + pe_ref[0]).astype(BF)
        xb = (x + pe_ref[1]).astype(BF)
        for g in range(NSA_GROUPS):
            first = _dot(xa, w1_ref[0, g])
            second = _dot(xb, w1_ref[1, g])
            hid = first + pltpu.roll(second, ncp - 1, 0)
            act = (hid * jax.nn.sigmoid(hid)).astype(BF)
            out = _dot(act, w2_ref[g])
            if is_v:
                vct_ref[g] = out.T.astype(BF)
            else:
                kc_ref[g] = (out + cpos_ref[g]).astype(BF)


def _compress(xk, xv, pek, pev, w1k, w1v, w2k, w2v, cpos):
    b, ncp, width = xk.shape
    x_spec = pl.BlockSpec((None, ncp, width), lambda i: (i, 0, 0))
    return pl.pallas_call(
        _compress_kernel,
        out_shape=[jax.ShapeDtypeStruct((b, NSA_GROUPS, ncp, LANES), BF),
                   jax.ShapeDtypeStruct((b, NSA_GROUPS, LANES, ncp), BF)],
        grid=(b,),
        in_specs=[x_spec, x_spec,
                  _const_spec(pek.shape), _const_spec(pev.shape),
                  _const_spec(w1k.shape), _const_spec(w1v.shape),
                  _const_spec(w2k.shape), _const_spec(w2v.shape), _const_spec(cpos.shape)],
        out_specs=[pl.BlockSpec((None, NSA_GROUPS, ncp, LANES), lambda i: (i, 0, 0, 0)),
                   pl.BlockSpec((None, NSA_GROUPS, LANES, ncp), lambda i: (i, 0, 0, 0))],
        compiler_params=pltpu.CompilerParams(
            dimension_semantics=("arbitrary",), vmem_limit_bytes=VMEM_LIMIT),
        name="compress",
    )(xk, xv, pek, pev, w1k, w1v, w2k, w2v, cpos)


def _cmp_kernel(nq_ref, kc_ref, vct_ref, mt_ref, o_ref, sf_ref, ob_ref, *, tq, topk):
    ncp = kc_ref.shape[1]
    ns = mt_ref.shape[0]
    t0 = pl.program_id(1) * tq
    cidx = lax.broadcasted_iota(jnp.int32, (ncp, tq), 0)
    tpos = lax.broadcasted_iota(jnp.int32, (ncp, tq), 1) + t0
    valid = (cidx * CMP_STRIDE + (CMP_LEN - 1)) <= tpos
    col_live = (lax.broadcasted_iota(jnp.int32, (1, tq), 1) + t0) >= CMP_LEN - 1
    lane = lax.broadcasted_iota(jnp.int32, (tq, LANES), 1)
    mt = mt_ref[...]

    blk = lax.broadcasted_iota(jnp.int32, (ns, tq), 0)
    cur = (lax.broadcasted_iota(jnp.int32, (ns, tq), 1) + t0) >> 6
    sub8 = lax.broadcasted_iota(jnp.int32, (8, tq), 0)

    for g in range(NSA_GROUPS):
        kc = kc_ref[g]
        vct = vct_ref[g]
        imp = jnp.zeros((ns, tq), F32)
        for r in range(NSA_REP):
            c = 2 * r + g
            q = nq_ref[:, c * LANES:(c + 1) * LANES]
            s = jnp.where(valid, _dot_nt(kc, q), NEG)
            m = jnp.max(s, axis=0, keepdims=True)
            e = jnp.exp2(s - m)
            p = e * jnp.where(col_live, 1.0 / jnp.sum(e, axis=0, keepdims=True), 0.0)
            p_hi = p.astype(BF)
            p_lo = (p - p_hi.astype(F32)).astype(BF)
            imp = imp + _dot(mt, p_hi) + _dot(mt, p_lo)
            o = _dot(vct, p_hi).T
            if g == 0:
                ob_ref[r] = o
            else:
                o_ref[:, r * LANES:(r + 1) * LANES] = jnp.where(lane < HALF, ob_ref[r], o).astype(o_ref.dtype)

        score = jnp.where(blk == 0, FORCE_SCORE,
                          jnp.where(blk == cur, FORCE_SCORE,
                                    jnp.where(blk == cur - 1, FORCE_SCORE,
                                              jnp.where(blk <= cur, imp, -1.0))))
        rows = [jnp.broadcast_to(score[i:i + 1, :], (8, tq)) for i in range(ns)]
        feats = []
        for kt in range(ns // 8):
            sc = score[8 * kt:8 * kt + 8, :]
            cnt = jnp.zeros((8, tq), F32)
            for i in range(ns):
                if i < 8 * kt:
                    beat = jnp.where(rows[i] >= sc, 1.0, 0.0)
                elif i >= 8 * kt + 8:
                    beat = jnp.where(rows[i] > sc, 1.0, 0.0)
                else:
                    beat = jnp.where(sub8 + 8 * kt > i,
                                     jnp.where(rows[i] >= sc, 1.0, 0.0),
                                     jnp.where(rows[i] > sc, 1.0, 0.0))
                cnt = cnt + beat
            feats.append(jnp.where(cnt < topk, 0.0, -SEL_MASK_BIG))
        feats.append(jnp.zeros((LANES - ns, tq), F32))
        sf_ref[g] = jnp.concatenate(feats, axis=0).T.astype(BF)


def _cmp_topk(nq, kc, vct, mt, *, tq, topk):
    b, t, _ = nq.shape
    ncp = kc.shape[2]
    return pl.pallas_call(
        functools.partial(_cmp_kernel, tq=tq, topk=topk),
        out_shape=[jax.ShapeDtypeStruct((b, t, NSA_REP * LANES), BF),
                   jax.ShapeDtypeStruct((b, NSA_GROUPS, t, LANES), BF)],
        grid=(b, t // tq),
        in_specs=[
            pl.BlockSpec((None, tq, 2 * NSA_REP * LANES), lambda i, j: (i, j, 0)),
            pl.BlockSpec((None, NSA_GROUPS, ncp, LANES), lambda i, j: (i, 0, 0, 0)),
            pl.BlockSpec((None, NSA_GROUPS, LANES, ncp), lambda i, j: (i, 0, 0, 0)),
            _const_spec(mt.shape),
        ],
        out_specs=[pl.BlockSpec((None, tq, NSA_REP * LANES), lambda i, j: (i, j, 0)),
                   pl.BlockSpec((None, NSA_GROUPS, tq, LANES), lambda i, j: (i, 0, j, 0))],
        scratch_shapes=[pltpu.VMEM((NSA_REP, tq, LANES), F32)],
        compiler_params=pltpu.CompilerParams(
            dimension_semantics=("arbitrary", "arbitrary"), vmem_limit_bytes=VMEM_LIMIT),
        name="cmp_topk",
    )(nq, kc, vct, mt)


def _nsa_flash_kernel(*refs, tq, window, selected):
    if selected:
        nq_ref, sf_ref, k_ref, vt_ref, o_ref, qs_ref, s_ref, acc_ref, m_ref, l_ref, ob_ref = refs
    else:
        nq_ref, k_ref, vt_ref, o_ref, qs_ref, s_ref, acc_ref, m_ref, l_ref, ob_ref = refs
    qi = pl.program_id(1)
    kd = qs_ref.shape[2]
    cols = NSA_REP * tq
    key = lax.broadcasted_iota(jnp.int32, (tq, tq), 0)
    qry = lax.broadcasted_iota(jnp.int32, (tq, tq), 1)
    ones = jnp.ones((BF16_ROWS, tq), BF)

    for g in range(NSA_GROUPS):
        for r in range(NSA_REP):
            c = 2 * r + g
            qs_ref[g, r * tq:(r + 1) * tq, 0:LANES] = nq_ref[:, c * LANES:(c + 1) * LANES]
            if selected:
                qs_ref[g, r * tq:(r + 1) * tq, LANES:2 * LANES] = sf_ref[g]

    m_ref[...] = jnp.full(m_ref.shape, NEG, F32)
    l_ref[...] = jnp.zeros(l_ref.shape, F32)
    acc_ref[...] = jnp.zeros(acc_ref.shape, F32)

    def scores(j, buf):
        ks = pl.multiple_of(j * tq, tq)
        for g in range(NSA_GROUPS):
            s_ref[buf, g] = _dot_nt(k_ref[pl.ds(ks, tq), g * kd:(g + 1) * kd], qs_ref[g])

    def consume(j, buf, mode, live=None):
        if mode == "edge":
            edge_qry = jnp.where(live, qry, tq)
        for g in range(NSA_GROUPS):
            vt = vt_ref[j, g * NSA_HEAD_DIM:(g + 1) * NSA_HEAD_DIM, :]
            for r in range(NSA_REP):
                sl = slice(r * tq, (r + 1) * tq)
                s = s_ref[buf, g, :, sl]
                if mode == "diag":
                    s = jnp.where(key <= qry, s, NEG)
                elif mode == "edge":
                    s = jnp.where(key > edge_qry, s, NEG)
                elif live is not None:
                    s = jnp.where(live, s, NEG)
                m_ref[g, :, sl], l_ref[g, :, sl], acc_ref[g, :, sl] = _softmax_step(
                    s, m_ref[g, :, sl], l_ref[g, :, sl], acc_ref[g, :, sl], vt, ones)

    if window:
        nback = WIN // tq
        for i in range(nback + 1):
            j = qi - nback + i
            scores(jnp.maximum(j, 0), i)
            if i == nback:
                consume(j, i, "diag")
            else:
                consume(jnp.maximum(j, 0), i, "edge" if i == 0 else "full", j >= 0)
    else:
        scores(0, 0)

        def pair(p, carry):
            scores(2 * p + 1, 1)
            consume(2 * p, 0, "full")
            scores(2 * p + 2, 0)
            consume(2 * p + 1, 1, "full")
            return carry

        lax.fori_loop(0, qi // 2, pair, 0)

        @pl.when(qi % 2 == 1)
        def _():
            scores(qi, 1)
            consume(qi - 1, 0, "full")
            consume(qi, 1, "diag")

        @pl.when(qi % 2 == 0)
        def _():
            consume(qi, 0, "diag")

    for g in range(NSA_GROUPS):
        o = acc_ref[g] / l_ref[g]
        for r in range(NSA_REP):
            ob_ref[r, g * NSA_HEAD_DIM:(g + 1) * NSA_HEAD_DIM, :] = o[:, r * tq:(r + 1) * tq]

    for r in range(NSA_REP):
        o_ref[:, r * LANES:(r + 1) * LANES] = ob_ref[r].T.astype(o_ref.dtype)


def _nsa_flash(nq, k, vt, sf=None, *, tq, window):
    b, t, _ = nq.shape
    selected = sf is not None
    kd = 2 * LANES if selected else LANES
    nk = t // tq
    assert vt.shape == (b, nk, LANES, tq) and k.shape == (b, t, NSA_GROUPS * kd)
    in_specs = [pl.BlockSpec((None, tq, 2 * NSA_REP * LANES), lambda i, j: (i, j, 0))]
    args = [nq]
    if selected:
        in_specs.append(pl.BlockSpec((None, NSA_GROUPS, tq, LANES), lambda i, j: (i, 0, j, 0)))
        args.append(sf)
    in_specs.append(pl.BlockSpec((None, t, NSA_GROUPS * kd), lambda i, j: (i, 0, 0)))
    args.append(k)
    in_specs.append(pl.BlockSpec((None, nk, LANES, tq), lambda i, j: (i, 0, 0, 0)))
    args.append(vt)
    cols = NSA_REP * tq
    return pl.pallas_call(
        functools.partial(_nsa_flash_kernel, tq=tq, window=window, selected=selected),
        out_shape=jax.ShapeDtypeStruct((b, t, NSA_REP * LANES), BF),
        grid=(b, t // tq),
        in_specs=in_specs,
        out_specs=pl.BlockSpec((None, tq, NSA_REP * LANES), lambda i, j: (i, j, 0)),
        scratch_shapes=[
            pltpu.VMEM((NSA_GROUPS, cols, kd), BF),
            pltpu.VMEM((WIN // tq + 1 if window else 2, NSA_GROUPS, tq, cols), F32),
            pltpu.VMEM((NSA_GROUPS, NSA_HEAD_DIM, cols), F32),
            pltpu.VMEM((NSA_GROUPS, 1, cols), F32),
            pltpu.VMEM((NSA_GROUPS, 1, cols), F32),
            pltpu.VMEM((NSA_REP, LANES, tq), F32),
        ],
        compiler_params=pltpu.CompilerParams(
            dimension_semantics=("arbitrary", "arbitrary"), vmem_limit_bytes=VMEM_LIMIT),
        name="nsa_window" if window else "nsa_selected",
    )(*args)


def _mixout_xa_kernel(x_ref, oa_ref, oc_ref, os_ref, ow_ref, gat_ref, e_ref, wa_ref, wb_ref, mpost_ref,
                      pre_ref, post_ref, wq_ref, wo_ref, k_ref, v_ref, o_ref, *, scale):
    gt = gat_ref[...]
    g_hi = gt.astype(BF)
    g_lo = (gt - g_hi.astype(F32)).astype(BF)
    g_split = jnp.concatenate([g_hi, g_lo], axis=1)
    ob = None
    for c, ref in enumerate((oc_ref, os_ref, ow_ref)):
        gate = _dot(g_split, e_ref[c])
        term = gate * ref[...].astype(F32)
        ob = term if ob is None else ob + term
    y = _dot(oa_ref[...], wa_ref[...]) + _dot(ob.astype(BF), wb_ref[...])
    x = x_ref[...] + _rms(y, mpost_ref[...])

    h = _rms(x, pre_ref[...]).astype(BF)
    q = (_dot(h, wq_ref[...]) * scale).astype(BF)
    hd = q.shape[1] // XA_HEADS
    outs = []
    for i in range(XA_HEADS):
        sl = slice(i * hd, (i + 1) * hd)
        s = _dot_nt(q[:, sl], k_ref[:, sl])
        m = jnp.max(s, axis=1, keepdims=True)
        e = jnp.exp(s - m)
        p = e / jnp.sum(e, axis=1, keepdims=True)
        outs.append(_dot(p.astype(BF), v_ref[:, sl]).astype(BF))
    y = _dot(jnp.concatenate(outs, axis=1), wo_ref[...])
    o_ref[...] = x + _rms(y, post_ref[...])


def _mixout_xa(x, oa, oc, osel, ow, gates, e, wa, wb, mix_post_g, pre_g, post_g, wq, wo, k, v, *, tm):
    b, t, d = x.shape
    nm = k.shape[1]
    hd = wq.shape[1] // XA_HEADS
    row = lambda w: pl.BlockSpec((None, tm, w), lambda i, j: (i, j, 0))
    kv = pl.BlockSpec((None, nm, k.shape[2]), lambda i, j: (i, 0, 0))
    vec = _const_spec((1, d))
    return pl.pallas_call(
        functools.partial(_mixout_xa_kernel, scale=float(hd) ** -0.5),
        out_shape=jax.ShapeDtypeStruct((b, t, d), F32),
        grid=(b, t // tm),
        in_specs=[row(d), row(oa.shape[2]), row(oc.shape[2]), row(osel.shape[2]), row(ow.shape[2]), row(LANES),
                  _const_spec(e.shape), _const_spec(wa.shape), _const_spec(wb.shape), vec,
                  vec, vec, _const_spec(wq.shape), _const_spec(wo.shape), kv, kv],
        out_specs=row(d),
        compiler_params=pltpu.CompilerParams(
            dimension_semantics=("arbitrary", "arbitrary"), vmem_limit_bytes=VMEM_LIMIT),
        name="mixout_xattn",
    )(x, oa, oc, osel, ow, gates, e, wa, wb, mix_post_g, pre_g, post_g, wq, wo, k, v)


def _memkv_kernel(m_ref, g_ref, wk_ref, wv_ref, k_ref, v_ref):
    m = _rms(m_ref[...], g_ref[...]).astype(BF)
    k_ref[...] = _dot(m, wk_ref[...]).astype(BF)
    v_ref[...] = _dot(m, wv_ref[...]).astype(BF)


def _memkv(mem, g, wk, wv):
    b, nm, d = mem.shape
    spec = pl.BlockSpec((None, nm, d), lambda i: (i, 0, 0))
    return pl.pallas_call(
        _memkv_kernel,
        out_shape=[jax.ShapeDtypeStruct((b, nm, d), BF)] * 2,
        grid=(b,),
        in_specs=[spec, _const_spec((1, d)), _const_spec(wk.shape), _const_spec(wv.shape)],
        out_specs=[spec, spec],
        compiler_params=pltpu.CompilerParams(
            dimension_semantics=("arbitrary",), vmem_limit_bytes=VMEM_LIMIT),
        name="mem_kv",
    )(mem, g, wk, wv)


def _bf16_terms(x, n):
    terms = []
    for _ in range(n):
        u = np.float32(x).view(np.uint32)
        hi = np.uint32((int(u) + 0x7FFF + ((int(u) >> 16) & 1)) & 0xFFFF0000).view(np.float32)
        terms.append(float(hi))
        x = float(np.float64(x) - np.float64(hi))
    return terms


def _query_feature_rows(slopes_by_chunk):
    out = np.zeros((1, len(slopes_by_chunk) * LANES), np.float32)
    for j, s in enumerate(slopes_by_chunk):
        base = j * LANES + (HALF if j % 2 == 0 else 0)
        c64 = _bf16_terms(s * 64.0 * LOG2E, POS_FEATS // 2)
        c1 = _bf16_terms(s * LOG2E, POS_FEATS // 2)
        for i in range(POS_FEATS // 2):
            out[0, base + 2 * i] = c64[i]
            out[0, base + 2 * i + 1] = c1[i]
    return jnp.asarray(out)


def _static_tables(t):
    ncp = t // CMP_STRIDE
    ns = t // SEL_LEN
    da = _alibi_slopes(DA_HEADS)
    qf_da = _query_feature_rows([da[j // 2] for j in range(2 * DA_HEADS)])
    nsa = _alibi_slopes(NSA_GROUPS * NSA_REP)
    qf_n = _query_feature_rows([nsa[(j % 2) * NSA_REP + j // 2] for j in range(2 * NSA_REP)])

    c_end = np.arange(ncp) * CMP_STRIDE + CMP_LEN - 1
    cpos = np.zeros((NSA_GROUPS, ncp, LANES), np.float32)
    for i in range(POS_FEATS // 2):
        cpos[0, :, HALF + 2 * i] = c_end // 64
        cpos[0, :, HALF + 2 * i + 1] = c_end % 64
        cpos[1, :, 2 * i] = c_end // 64
        cpos[1, :, 2 * i + 1] = c_end % 64

    c_start = np.arange(ncp) * CMP_STRIDE
    s_start = np.arange(ns) * SEL_LEN
    overlap = np.clip(np.minimum(c_start[:, None] + CMP_LEN, s_start[None, :] + SEL_LEN)
                      - np.maximum(c_start[:, None], s_start[None, :]), 0, None)
    mt = (overlap.astype(np.float32) / CMP_LEN).T
    mt[:, ncp - 1] = 0.0

    e = np.zeros((3, LANES, NSA_REP * LANES), np.float32)
    for g in range(NSA_GROUPS):
        for r in range(NSA_REP):
            for c in range(3):
                col = r * LANES + g * HALF
                e[c, g * NSA_REP * 3 + r * 3 + c, col:col + HALF] = 1.0
    e = np.concatenate([e, e], axis=1)
    return qf_da, qf_n, jnp.asarray(cpos), jnp.asarray(mt, BF), jnp.asarray(e, BF)


def _compress_weights(pe, w1, w2):
    hidden = w1.shape[1]
    w1r = w1.reshape(2, CMP_STRIDE, NSA_HEAD_DIM, hidden)
    w1g = jnp.zeros((2, NSA_GROUPS, CMP_STRIDE, NSA_GROUPS, NSA_HEAD_DIM, hidden), w1.dtype)
    for g in range(NSA_GROUPS):
        w1g = w1g.at[:, g, :, g].set(w1r)
    w1g = w1g.reshape(2, NSA_GROUPS, CMP_STRIDE * LANES, hidden).astype(BF)
    per = pe.reshape(2, CMP_STRIDE, 1, NSA_HEAD_DIM)
    peg = jnp.broadcast_to(per, (2, CMP_STRIDE, NSA_GROUPS, NSA_HEAD_DIM)).reshape(2, 1, CMP_STRIDE * LANES)
    w2g = jnp.zeros((NSA_GROUPS, hidden, NSA_GROUPS, NSA_HEAD_DIM), w2.dtype)
    for g in range(NSA_GROUPS):
        w2g = w2g.at[g, :, g].set(w2)
    w2g = w2g.reshape(NSA_GROUPS, hidden, LANES).astype(BF)
    return peg, w1g, w2g


def kernel(x, mem, ffn1_pre_g, ffn1_post_g, ffn1_w_gate, ffn1_w_up, ffn1_w_down, mix_pre_g, mix_post_g,
           w_mix_in, da_lambda_q1, da_lambda_k1, da_lambda_q2, da_lambda_k2, da_subln_g, cmp_k_pe,
           cmp_k_w1, cmp_k_w2, cmp_v_pe, cmp_v_w1, cmp_v_w2, w_mix_out, xa_pre_g, xa_post_g, mem_norm_g,
           xa_w_q, xa_w_k, xa_w_v, xa_w_o, ffn2_pre_g, ffn2_post_g, ffn2_w_gate, ffn2_w_up, ffn2_w_down):
    b, t, d = x.shape
    depth = ffn1_pre_g.shape[0]
    n = b * t
    tm = 512
    tq_da = 512
    tq = 256
    ns = t // SEL_LEN
    assert t % tm == 0 and ns <= LANES and WIN % tq == 0
    assert d == DA_HEADS * DA_V_DIM + NSA_GROUPS * NSA_REP * NSA_HEAD_DIM
    qf_da, qf_n, cpos, mt, e = _static_tables(t)
    row = lambda v: v.reshape(1, -1)

    for l in range(depth):
        x2 = _ffn(x.reshape(n, d), row(ffn1_pre_g[l]), row(ffn1_post_g[l]),
                  ffn1_w_gate[l].astype(BF), ffn1_w_up[l].astype(BF), ffn1_w_down[l].astype(BF), tm=tm)

        w = w_mix_in[l]
        dq = DA_HEADS * 2 * DA_QK_DIM
        dv = DA_HEADS * DA_V_DIM
        nqw = NSA_GROUPS * NSA_REP * NSA_HEAD_DIM
        nkv = NSA_GROUPS * NSA_HEAD_DIM
        o0 = 2 * dq + dv
        w_nq = w[:, o0:o0 + nqw].reshape(d, NSA_GROUPS, NSA_REP, NSA_HEAD_DIM).transpose(0, 2, 1, 3).reshape(d, nqw)
        o1 = o0 + nqw
        seg = lambda i: w[:, o1 + i * nkv:o1 + (i + 1) * nkv]
        w_g = w[:, o1 + 6 * nkv:]
        w_g = jnp.pad(w_g, ((0, 0), (0, LANES - w_g.shape[1])))
        w_big = jnp.concatenate([w[:, :o0], w_nq, seg(2), seg(4), seg(3), seg(5), seg(0), seg(1), w_g],
                                axis=1).astype(BF)
        (daq, dak, davt, nq, nks, nkw, nvst, nvwt, nkc, nvc, gates) = _mixin(
            x2.reshape(b, t, d), row(mix_pre_g[l]), w_big, qf_da, qf_n, tm=tm, tk_da=tq_da, tk_nsa=tq)

        lam_init = 0.8 - 0.6 * float(np.exp(-0.3 * l))
        o_a = _da_attention(daq, dak, davt, row(da_lambda_q1[l]), row(da_lambda_k1[l]),
                            row(da_lambda_q2[l]), row(da_lambda_k2[l]), row(da_subln_g[l]),
                            lam_init=lam_init, tq=tq_da)

        pek, w1k, w2k = _compress_weights(cmp_k_pe[l], cmp_k_w1[l], cmp_k_w2[l])
        pev, w1v, w2v = _compress_weights(cmp_v_pe[l], cmp_v_w1[l], cmp_v_w2[l])
        ncp = t // CMP_STRIDE
        kc, vct = _compress(nkc.reshape(b, ncp, CMP_STRIDE * LANES), nvc.reshape(b, ncp, CMP_STRIDE * LANES),
                            pek, pev, w1k, w1v, w2k, w2v, cpos)
        o_cmp, sf = _cmp_topk(nq, kc, vct, mt, tq=tq, topk=min(SEL_TOPK, ns))
        o_sel = _nsa_flash(nq, nks, nvst, sf, tq=tq, window=False)
        o_win = _nsa_flash(nq, nkw, nvwt, tq=tq, window=True)

        wo = w_mix_out[l]
        wa = wo[:dv].astype(BF)
        wb = wo[dv:].reshape(NSA_GROUPS, NSA_REP, NSA_HEAD_DIM, d).transpose(1, 0, 2, 3).reshape(nqw, d).astype(BF)
        mk, mv = _memkv(mem, row(mem_norm_g[l]), xa_w_k[l].astype(BF), xa_w_v[l].astype(BF))
        x4 = _mixout_xa(x2.reshape(b, t, d), o_a, o_cmp, o_sel, o_win, gates, e, wa, wb, row(mix_post_g[l]),
                        row(xa_pre_g[l]), row(xa_post_g[l]), xa_w_q[l].astype(BF), xa_w_o[l].astype(BF),
                        mk, mv, tm=tm)

        x = _ffn(x4.reshape(n, d), row(ffn2_pre_g[l]), row(ffn2_post_g[l]),
                 ffn2_w_gate[l].astype(BF), ffn2_w_up[l].astype(BF), ffn2_w_down[l].astype(BF),
                 tm=tm).reshape(b, t, d)
    return x
```

```python
import functools

import numpy as np
import jax
import jax.numpy as jnp
from jax import lax
from jax.experimental import pallas as pl
from jax.experimental.pallas import tpu as pltpu

BF = jnp.bfloat16
F32 = jnp.float32

EPS = 1e-6
NEG = -1e30
LANES = 128
HALF = LANES // 2
BF16_ROWS = 16

DA_HEADS = 4
DA_QK_DIM = 64
DA_V_DIM = 128
NSA_GROUPS = 2
NSA_REP = 4
NSA_HEAD_DIM = 64
CMP_LEN = 32
CMP_STRIDE = 16
SEL_LEN = 64
SEL_TOPK = 16
WIN = 512
FORCE_SCORE = 1e4
XA_HEADS = 4
SEL_MASK_BIG = 32768.0
LOG2E = 1.4426950408889634
QK_SCALE_LOG2 = 0.125 * LOG2E
POS_FEATS = 6

VMEM_LIMIT = 56 * 1024 * 1024

_NT = (((1,), (1,)), ((), ()))


def _rms(x, g):
    ms = jnp.mean(x * x, axis=-1, keepdims=True)
    return x * lax.rsqrt(ms + EPS) * g


def _dot(a, b):
    return jnp.dot(a, b, preferred_element_type=F32)


def _dot_nt(a, b):
    return lax.dot_general(a, b, _NT, preferred_element_type=F32)


def _const_spec(shape):
    nd = len(shape)
    return pl.BlockSpec(shape, lambda *_: (0,) * nd, pipeline_mode=pl.Buffered(1))


def _alibi_slopes(n):
    return [2.0 ** (-8.0 * (i + 1) / n) for i in range(n)]


def _softmax_step(s, m_old, l_old, acc_old, vt, ones):
    m_new = jnp.maximum(m_old, jnp.max(s, axis=0, keepdims=True))
    alpha = jnp.exp2(m_old - m_new)
    p = jnp.exp2(s - m_new).astype(BF)
    l_new = alpha * l_old + _dot(ones, p)[0:1]
    acc_new = alpha * acc_old + _dot(vt, p)
    return m_new, l_new, acc_new


def _ffn_kernel(x_ref, pre_ref, post_ref, wg_ref, wu_ref, wd_ref, o_ref):
    x = x_ref[...]
    h = _rms(x, pre_ref[...]).astype(BF)
    g = _dot(h, wg_ref[...])
    u = _dot(h, wu_ref[...])
    a = (g * jax.nn.sigmoid(g) * u).astype(BF)
    y = _dot(a, wd_ref[...])
    o_ref[...] = x + 0.5 * _rms(y, post_ref[...])


def _ffn(x2, pre_g, post_g, wg, wu, wd, *, tm):
    n, d = x2.shape
    f = wg.shape[1]
    return pl.pallas_call(
        _ffn_kernel,
        out_shape=jax.ShapeDtypeStruct((n, d), F32),
        grid=(n // tm,),
        in_specs=[
            pl.BlockSpec((tm, d), lambda i: (i, 0)),
            _const_spec((1, d)), _const_spec((1, d)),
            _const_spec((d, f)), _const_spec((d, f)), _const_spec((f, d)),
        ],
        out_specs=pl.BlockSpec((tm, d), lambda i: (i, 0)),
        compiler_params=pltpu.CompilerParams(
            dimension_semantics=("arbitrary",), vmem_limit_bytes=VMEM_LIMIT),
        name="ffn",
    )(x2, pre_g, post_g, wg, wu, wd)


def _mixin_kernel(x_ref, g_ref, w_ref, qfda_ref, qfn_ref,
                  daq, dak, davt, nq, nks, nkw, nvst, nvwt, nkc, nvc, gat, *, tk_da, tk_nsa):
    tm = x_ref.shape[0]
    t0 = pl.program_id(1) * tm
    h = _rms(x_ref[...], g_ref[...]).astype(BF)
    z = _dot(h, w_ref[...])

    lane = lax.broadcasted_iota(jnp.int32, (tm, LANES), 1)
    row = lax.broadcasted_iota(jnp.int32, (tm, LANES), 0) + t0
    pa = (row >> 6).astype(F32)
    pb = (row & 63).astype(F32)
    low = lane < HALF
    pab = jnp.where((lane & 1) == 0, pa, pb)
    pos_lo = jnp.where(lane < HALF, 0.0, jnp.where(lane < HALF + POS_FEATS, pab, 0.0))
    pos_hi = jnp.where(lane < POS_FEATS, pab, 0.0)
    blk_onehot = jnp.where(lane == (row >> 6), 1.0, 0.0)

    def ch(i):
        return z[:, i * LANES:(i + 1) * LANES]

    def put(ref, j, val):
        ref[:, j * LANES:(j + 1) * LANES] = val.astype(ref.dtype)

    for c in range(4):
        q = ch(c) * QK_SCALE_LOG2
        put(daq, 2 * c, jnp.where(low, q, qfda_ref[:, (2 * c) * LANES:(2 * c + 1) * LANES]))
        put(daq, 2 * c + 1, jnp.where(low, qfda_ref[:, (2 * c + 1) * LANES:(2 * c + 2) * LANES], q))
        k = ch(4 + c)
        put(dak, 2 * c, jnp.where(low, k, pos_lo))
        put(dak, 2 * c + 1, jnp.where(low, pos_hi, k))
        vt = ch(8 + c).T
        for s in range(tm // tk_da):
            davt[s, c * LANES:(c + 1) * LANES, :] = vt[:, s * tk_da:(s + 1) * tk_da].astype(BF)
        q = ch(12 + c) * QK_SCALE_LOG2
        put(nq, 2 * c, jnp.where(low, q, qfn_ref[:, (2 * c) * LANES:(2 * c + 1) * LANES]))
        put(nq, 2 * c + 1, jnp.where(low, qfn_ref[:, (2 * c + 1) * LANES:(2 * c + 2) * LANES], q))
    k = ch(16)
    put(nks, 0, jnp.where(low, k, pos_lo))
    put(nks, 1, blk_onehot)
    put(nks, 2, jnp.where(low, pos_hi, k))
    put(nks, 3, blk_onehot)
    k = ch(17)
    put(nkw, 0, jnp.where(low, k, pos_lo))
    put(nkw, 1, jnp.where(low, pos_hi, k))
    for src, ref in ((18, nvst), (19, nvwt)):
        vt = ch(src).T
        for s in range(tm // tk_nsa):
            ref[s] = vt[:, s * tk_nsa:(s + 1) * tk_nsa].astype(BF)
    nkc[...] = ch(20)
    nvc[...] = ch(21)
    gat[...] = jax.nn.sigmoid(ch(22))


def _mixin(x, g, w, qf_da, qf_n, *, tm, tk_da, tk_nsa):
    b, t, d = x.shape
    wcols = w.shape[1]
    assert tm % tk_da == 0 and tm % tk_nsa == 0
    row = lambda wd, dt: (jax.ShapeDtypeStruct((b, t, wd), dt),
                          pl.BlockSpec((None, tm, wd), lambda i, j: (i, j, 0)))
    tr = lambda rows, tk: (jax.ShapeDtypeStruct((b, t // tk, rows, tk), BF),
                           pl.BlockSpec((None, tm // tk, rows, tk), lambda i, j: (i, j, 0, 0)))
    outs = [row(1024, BF), row(1024, BF), tr(512, tk_da), row(1024, BF), row(512, BF), row(256, BF),
            tr(LANES, tk_nsa), tr(LANES, tk_nsa), row(LANES, F32), row(LANES, F32), row(LANES, F32)]
    return pl.pallas_call(
        functools.partial(_mixin_kernel, tk_da=tk_da, tk_nsa=tk_nsa),
        out_shape=[o[0] for o in outs],
        grid=(b, t // tm),
        in_specs=[
            pl.BlockSpec((None, tm, d), lambda i, j: (i, j, 0)),
            _const_spec((1, d)), _const_spec((d, wcols)),
            _const_spec((1, 1024)), _const_spec((1, 1024)),
        ],
        out_specs=[o[1] for o in outs],
        compiler_params=pltpu.CompilerParams(
            dimension_semantics=("arbitrary", "arbitrary"), vmem_limit_bytes=VMEM_LIMIT),
        name="mixin",
    )(x, g, w, qf_da, qf_n)


def _da_kernel(q_ref, k_ref, vt_ref, lq1, lk1, lq2, lk2, sg_ref, o_ref, s_ref, acc_ref, m_ref, l_ref,
               *, lam_init, tq, strip):
    qi = pl.program_id(2)
    m_ref[...] = jnp.full(m_ref.shape, NEG, F32)
    l_ref[...] = jnp.zeros(l_ref.shape, F32)
    acc_ref[...] = jnp.zeros(acc_ref.shape, F32)

    def scores(j, buf):
        ks = pl.multiple_of(j * tq, tq)
        for mp in range(2):
            lanes = slice(mp * LANES, (mp + 1) * LANES)
            s_ref[buf, mp] = _dot_nt(k_ref[pl.ds(ks, tq), lanes], q_ref[:, lanes])

    def consume(j, buf, diag=False):
        for mp in range(2):
            for c in range(tq // strip):
                sl = slice(c * strip, (c + 1) * strip)
                nk = (c + 1) * strip if diag else tq
                s = s_ref[buf, mp, 0:nk, sl]
                if diag:
                    kk = lax.broadcasted_iota(jnp.int32, (nk, strip), 0)
                    qq = lax.broadcasted_iota(jnp.int32, (nk, strip), 1) + c * strip
                    s = jnp.where(kk <= qq, s, NEG)
                m_ref[mp, :, sl], l_ref[mp, :, sl], acc_ref[mp, :, sl] = _softmax_step(
                    s, m_ref[mp, :, sl], l_ref[mp, :, sl], acc_ref[mp, :, sl],
                    vt_ref[j, :, 0:nk], jnp.ones((BF16_ROWS, nk), BF))

    scores(0, 0)

    def pair(p, carry):
        scores(2 * p + 1, 1)
        consume(2 * p, 0)
        scores(2 * p + 2, 0)
        consume(2 * p + 1, 1)
        return carry

    lax.fori_loop(0, qi // 2, pair, 0)

    @pl.when(qi % 2 == 1)
    def _():
        scores(qi, 1)
        consume(qi - 1, 0)
        consume(qi, 1, diag=True)

    @pl.when(qi % 2 == 0)
    def _():
        consume(qi, 0, diag=True)

    lam = (jnp.exp(jnp.sum(lq1[...] * lk1[...], axis=1, keepdims=True))
           - jnp.exp(jnp.sum(lq2[...] * lk2[...], axis=1, keepdims=True)) + lam_init)
    o = (acc_ref[0] / l_ref[0] - lam * (acc_ref[1] / l_ref[1])).T
    o_ref[...] = (_rms(o, sg_ref[...]) * (1.0 - lam_init)).astype(o_ref.dtype)


def _da_attention(daq, dak, davt, lq1, lk1, lq2, lk2, subln_g, *, lam_init, tq):
    b, t, _ = daq.shape
    nk = t // tq
    vec = _const_spec((1, DA_QK_DIM))
    return pl.pallas_call(
        functools.partial(_da_kernel, lam_init=lam_init, tq=tq, strip=2 * LANES),
        out_shape=jax.ShapeDtypeStruct((b, t, DA_HEADS * DA_V_DIM), BF),
        grid=(b, DA_HEADS, t // tq),
        in_specs=[
            pl.BlockSpec((None, tq, 2 * LANES), lambda i, h, j: (i, j, h)),
            pl.BlockSpec((None, t, 2 * LANES), lambda i, h, j: (i, 0, h)),
            pl.BlockSpec((None, nk, DA_V_DIM, tq), lambda i, h, j: (i, 0, h, 0)),
            vec, vec, vec, vec, _const_spec((1, DA_V_DIM)),
        ],
        out_specs=pl.BlockSpec((None, tq, DA_V_DIM), lambda i, h, j: (i, j, h)),
        scratch_shapes=[
            pltpu.VMEM((2, 2, tq, tq), F32),
            pltpu.VMEM((2, DA_V_DIM, tq), F32),
            pltpu.VMEM((2, 1, tq), F32),
            pltpu.VMEM((2, 1, tq), F32),
        ],
        compiler_params=pltpu.CompilerParams(
            dimension_semantics=("arbitrary", "arbitrary", "arbitrary"), vmem_limit_bytes=VMEM_LIMIT),
        name="diff_attn",
    )(daq, dak, davt, lq1, lk1, lq2, lk2, subln_g)


def _compress_kernel(xk_ref, xv_ref, pek_ref, pev_ref, w1k_ref, w1v_ref, w2k_ref, w2v_ref, cpos_ref,
                     kc_ref, vct_ref):
    ncp = xk_ref.shape[0]
    for x_ref, pe_ref, w1_ref, w2_ref, is_v in ((xk_ref, pek_ref, w1k_ref, w2k_ref, False),
                                                (xv_ref, pev_ref, w1v_ref, w2v_ref, True)):
        x = x_ref[...]
        xa = (x + pe_ref[0]).astype(BF)
        xb = (x + pe_ref[1]).astype(BF)
        for g in range(NSA_GROUPS):
            first = _dot(xa, w1_ref[0, g])
            second = _dot(xb, w1_ref[1, g])
            hid = first + pltpu.roll(second, ncp - 1, 0)
            act = (hid * jax.nn.sigmoid(hid)).astype(BF)
            out = _dot(act, w2_ref[g])
            if is_v:
                vct_ref[g] = out.T.astype(BF)
            else:
                kc_ref[g] = (out + cpos_ref[g]).astype(BF)


def _compress(xk, xv, pek, pev, w1k, w1v, w2k, w2v, cpos):
    b, ncp, width = xk.shape
    x_spec = pl.BlockSpec((None, ncp, width), lambda i: (i, 0, 0))
    return pl.pallas_call(
        _compress_kernel,
        out_shape=[jax.ShapeDtypeStruct((b, NSA_GROUPS, ncp, LANES), BF),
                   jax.ShapeDtypeStruct((b, NSA_GROUPS, LANES, ncp), BF)],
        grid=(b,),
        in_specs=[x_spec, x_spec,
                  _const_spec(pek.shape), _const_spec(pev.shape),
                  _const_spec(w1k.shape), _const_spec(w1v.shape),
                  _const_spec(w2k.shape), _const_spec(w2v.shape), _const_spec(cpos.shape)],
        out_specs=[pl.BlockSpec((None, NSA_GROUPS, ncp, LANES), lambda i: (i, 0, 0, 0)),
                   pl.BlockSpec((None, NSA_GROUPS, LANES, ncp), lambda i: (i, 0, 0, 0))],
        compiler_params=pltpu.CompilerParams(
            dimension_semantics=("arbitrary",), vmem_limit_bytes=VMEM_LIMIT),
        name="compress",
    )(xk, xv, pek, pev, w1k, w1v, w2k, w2v, cpos)


def _cmp_kernel(nq_ref, kc_ref, vct_ref, mt_ref, o_ref, sf_ref, ob_ref, *, tq, topk):
    ncp = kc_ref.shape[1]
    ns = mt_ref.shape[0]
    t0 = pl.program_id(1) * tq
    cidx = lax.broadcasted_iota(jnp.int32, (ncp, tq), 0)
    tpos = lax.broadcasted_iota(jnp.int32, (ncp, tq), 1) + t0
    valid = (cidx * CMP_STRIDE + (CMP_LEN - 1)) <= tpos
    col_live = (lax.broadcasted_iota(jnp.int32, (1, tq), 1) + t0) >= CMP_LEN - 1
    lane = lax.broadcasted_iota(jnp.int32, (tq, LANES), 1)
    mt = mt_ref[...]

    blk = lax.broadcasted_iota(jnp.int32, (ns, tq), 0)
    cur = (lax.broadcasted_iota(jnp.int32, (ns, tq), 1) + t0) >> 6
    sub8 = lax.broadcasted_iota(jnp.int32, (8, tq), 0)

    for g in range(NSA_GROUPS):
        kc = kc_ref[g]
        vct = vct_ref[g]
        imp = jnp.zeros((ns, tq), F32)
        for r in range(NSA_REP):
            c = 2 * r + g
            q = nq_ref[:, c * LANES:(c + 1) * LANES]
            s = jnp.where(valid, _dot_nt(kc, q), NEG)
            m = jnp.max(s, axis=0, keepdims=True)
            e = jnp.exp2(s - m)
            p = e * jnp.where(col_live, 1.0 / jnp.sum(e, axis=0, keepdims=True), 0.0)
            p_hi = p.astype(BF)
            imp = imp + _dot(mt, p_hi)
            o = _dot(vct, p_hi).T
            if g == 0:
                ob_ref[r] = o
            else:
                o_ref[:, r * LANES:(r + 1) * LANES] = jnp.where(lane < HALF, ob_ref[r], o).astype(o_ref.dtype)

        score = jnp.where(blk == 0, FORCE_SCORE,
                          jnp.where(blk == cur, FORCE_SCORE,
                                    jnp.where(blk == cur - 1, FORCE_SCORE,
                                              jnp.where(blk <= cur, imp, -1.0))))
        rows = [jnp.broadcast_to(score[i:i + 1, :], (8, tq)) for i in range(ns)]
        feats = []
        for kt in range(ns // 8):
            sc = score[8 * kt:8 * kt + 8, :]
            cnt = jnp.zeros((8, tq), F32)
            for i in range(ns):
                if i < 8 * kt:
                    beat = jnp.where(rows[i] >= sc, 1.0, 0.0)
                elif i >= 8 * kt + 8:
                    beat = jnp.where(rows[i] > sc, 1.0, 0.0)
                else:
                    beat = jnp.where(sub8 + 8 * kt > i,
                                     jnp.where(rows[i] >= sc, 1.0, 0.0),
                                     jnp.where(rows[i] > sc, 1.0, 0.0))
                cnt = cnt + beat
            feats.append(jnp.where(cnt < topk, 0.0, -SEL_MASK_BIG))
        feats.append(jnp.zeros((LANES - ns, tq), F32))
        sf_ref[g] = jnp.concatenate(feats, axis=0).T.astype(BF)


def _cmp_topk(nq, kc, vct, mt, *, tq, topk):
    b, t, _ = nq.shape
    ncp = kc.shape[2]
    return pl.pallas_call(
        functools.partial(_cmp_kernel, tq=tq, topk=topk),
        out_shape=[jax.ShapeDtypeStruct((b, t, NSA_REP * LANES), BF),
                   jax.ShapeDtypeStruct((b, NSA_GROUPS, t, LANES), BF)],
        grid=(b, t // tq),
        in_specs=[
            pl.BlockSpec((None, tq, 2 * NSA_REP * LANES), lambda i, j: (i, j, 0)),
            pl.BlockSpec((None, NSA_GROUPS, ncp, LANES), lambda i, j: (i, 0, 0, 0)),
            pl.BlockSpec((None, NSA_GROUPS, LANES, ncp), lambda i, j: (i, 0, 0, 0)),
            _const_spec(mt.shape),
        ],
        out_specs=[pl.BlockSpec((None, tq, NSA_REP * LANES), lambda i, j: (i, j, 0)),
                   pl.BlockSpec((None, NSA_GROUPS, tq, LANES), lambda i, j: (i, 0, j, 0))],
        scratch_shapes=[pltpu.VMEM((NSA_REP, tq, LANES), F32)],
        compiler_params=pltpu.CompilerParams(
            dimension_semantics=("arbitrary", "arbitrary"), vmem_limit_bytes=VMEM_LIMIT),
        name="cmp_topk",
    )(nq, kc, vct, mt)


def _nsa_flash_kernel(*refs, tq, window, selected):
    if selected:
        nq_ref, sf_ref, k_ref, vt_ref, o_ref, qs_ref, s_ref, acc_ref, m_ref, l_ref, ob_ref = refs
    else:
        nq_ref, k_ref, vt_ref, o_ref, qs_ref, s_ref, acc_ref, m_ref, l_ref, ob_ref = refs
    qi = pl.program_id(1)
    kd = qs_ref.shape[2]
    cols = NSA_REP * tq
    key = lax.broadcasted_iota(jnp.int32, (tq, tq), 0)
    qry = lax.broadcasted_iota(jnp.int32, (tq, tq), 1)
    ones = jnp.ones((BF16_ROWS, tq), BF)

    for g in range(NSA_GROUPS):
        for r in range(NSA_REP):
            c = 2 * r + g
            qs_ref[g, r * tq:(r + 1) * tq, 0:LANES] = nq_ref[:, c * LANES:(c + 1) * LANES]
            if selected:
                qs_ref[g, r * tq:(r + 1) * tq, LANES:2 * LANES] = sf_ref[g]

    m_ref[...] = jnp.full(m_ref.shape, NEG, F32)
    l_ref[...] = jnp.zeros(l_ref.shape, F32)
    acc_ref[...] = jnp.zeros(acc_ref.shape, F32)

    def scores(j, buf):
        ks = pl.multiple_of(j * tq, tq)
        for g in range(NSA_GROUPS):
            s_ref[buf, g] = _dot_nt(k_ref[pl.ds(ks, tq), g * kd:(g + 1) * kd], qs_ref[g])

    def consume(j, buf, mode, live=None):
        if mode == "edge":
            edge_qry = jnp.where(live, qry, tq)
        for g in range(NSA_GROUPS):
            vt = vt_ref[j, g * NSA_HEAD_DIM:(g + 1) * NSA_HEAD_DIM, :]
            for r in range(NSA_REP):
                sl = slice(r * tq, (r + 1) * tq)
                s = s_ref[buf, g, :, sl]
                if mode == "diag":
                    s = jnp.where(key <= qry, s, NEG)
                elif mode == "edge":
                    s = jnp.where(key > edge_qry, s, NEG)
                elif live is not None:
                    s = jnp.where(live, s, NEG)
                m_ref[g, :, sl], l_ref[g, :, sl], acc_ref[g, :, sl] = _softmax_step(
                    s, m_ref[g, :, sl], l_ref[g, :, sl], acc_ref[g, :, sl], vt, ones)

    if window:
        nback = WIN // tq
        for i in range(nback + 1):
            j = qi - nback + i
            scores(jnp.maximum(j, 0), i)
            if i == nback:
                consume(j, i, "diag")
            else:
                consume(jnp.maximum(j, 0), i, "edge" if i == 0 else "full", j >= 0)
    else:
        scores(0, 0)

        def pair(p, carry):
            scores(2 * p + 1, 1)
            consume(2 * p, 0, "full")
            scores(2 * p + 2, 0)
            consume(2 * p + 1, 1, "full")
            return carry

        lax.fori_loop(0, qi // 2, pair, 0)

        @pl.when(qi % 2 == 1)
        def _():
            scores(qi, 1)
            consume(qi - 1, 0, "full")
            consume(qi, 1, "diag")

        @pl.when(qi % 2 == 0)
        def _():
            consume(qi, 0, "diag")

    for g in range(NSA_GROUPS):
        o = acc_ref[g] / l_ref[g]
        for r in range(NSA_REP):
            ob_ref[r, g * NSA_HEAD_DIM:(g + 1) * NSA_HEAD_DIM, :] = o[:, r * tq:(r + 1) * tq]

    for r in range(NSA_REP):
        o_ref[:, r * LANES:(r + 1) * LANES] = ob_ref[r].T.astype(o_ref.dtype)


def _nsa_flash(nq, k, vt, sf=None, *, tq, window):
    b, t, _ = nq.shape
    selected = sf is not None
    kd = 2 * LANES if selected else LANES
    nk = t // tq
    assert vt.shape == (b, nk, LANES, tq) and k.shape == (b, t, NSA_GROUPS * kd)
    in_specs = [pl.BlockSpec((None, tq, 2 * NSA_REP * LANES), lambda i, j: (i, j, 0))]
    args = [nq]
    if selected:
        in_specs.append(pl.BlockSpec((None, NSA_GROUPS, tq, LANES), lambda i, j: (i, 0, j, 0)))
        args.append(sf)
    in_specs.append(pl.BlockSpec((None, t, NSA_GROUPS * kd), lambda i, j: (i, 0, 0)))
    args.append(k)
    in_specs.append(pl.BlockSpec((None, nk, LANES, tq), lambda i, j: (i, 0, 0, 0)))
    args.append(vt)
    cols = NSA_REP * tq
    return pl.pallas_call(
        functools.partial(_nsa_flash_kernel, tq=tq, window=window, selected=selected),
        out_shape=jax.ShapeDtypeStruct((b, t, NSA_REP * LANES), BF),
        grid=(b, t // tq),
        in_specs=in_specs,
        out_specs=pl.BlockSpec((None, tq, NSA_REP * LANES), lambda i, j: (i, j, 0)),
        scratch_shapes=[
            pltpu.VMEM((NSA_GROUPS, cols, kd), BF),
            pltpu.VMEM((WIN // tq + 1 if window else 2, NSA_GROUPS, tq, cols), F32),
            pltpu.VMEM((NSA_GROUPS, NSA_HEAD_DIM, cols), F32),
            pltpu.VMEM((NSA_GROUPS, 1, cols), F32),
            pltpu.VMEM((NSA_GROUPS, 1, cols), F32),
            pltpu.VMEM((NSA_REP, LANES, tq), F32),
        ],
        compiler_params=pltpu.CompilerParams(
            dimension_semantics=("arbitrary", "arbitrary"), vmem_limit_bytes=VMEM_LIMIT),
        name="nsa_window" if window else "nsa_selected",
    )(*args)


def _mixout_xa_kernel(x_ref, oa_ref, oc_ref, os_ref, ow_ref, gat_ref, e_ref, wa_ref, wb_ref, mpost_ref,
                      pre_ref, post_ref, wq_ref, wo_ref, k_ref, v_ref, o_ref, *, scale):
    gt = gat_ref[...]
    g_hi = gt.astype(BF)
    g_lo = (gt - g_hi.astype(F32)).astype(BF)
    g_split = jnp.concatenate([g_hi, g_lo], axis=1)
    ob = None
    for c, ref in enumerate((oc_ref, os_ref, ow_ref)):
        gate = _dot(g_split, e_ref[c])
        term = gate * ref[...].astype(F32)
        ob = term if ob is None else ob + term
    y = _dot(oa_ref[...], wa_ref[...]) + _dot(ob.astype(BF), wb_ref[...])
    x = x_ref[...] + _rms(y, mpost_ref[...])

    h = _rms(x, pre_ref[...]).astype(BF)
    q = (_dot(h, wq_ref[...]) * scale).astype(BF)
    hd = q.shape[1] // XA_HEADS
    outs = []
    for i in range(XA_HEADS):
        sl = slice(i * hd, (i + 1) * hd)
        s = _dot_nt(q[:, sl], k_ref[:, sl])
        m = jnp.max(s, axis=1, keepdims=True)
        e = jnp.exp(s - m)
        p = e / jnp.sum(e, axis=1, keepdims=True)
        outs.append(_dot(p.astype(BF), v_ref[:, sl]).astype(BF))
    y = _dot(jnp.concatenate(outs, axis=1), wo_ref[...])
    o_ref[...] = x + _rms(y, post_ref[...])


def _mixout_xa(x, oa, oc, osel, ow, gates, e, wa, wb, mix_post_g, pre_g, post_g, wq, wo, k, v, *, tm):
    b, t, d = x.shape
    nm = k.shape[1]
    hd = wq.shape[1] // XA_HEADS
    row = lambda w: pl.BlockSpec((None, tm, w), lambda i, j: (i, j, 0))
    kv = pl.BlockSpec((None, nm, k.shape[2]), lambda i, j: (i, 0, 0))
    vec = _const_spec((1, d))
    return pl.pallas_call(
        functools.partial(_mixout_xa_kernel, scale=float(hd) ** -0.5),
        out_shape=jax.ShapeDtypeStruct((b, t, d), F32),
        grid=(b, t // tm),
        in_specs=[row(d), row(oa.shape[2]), row(oc.shape[2]), row(osel.shape[2]), row(ow.shape[2]), row(LANES),
                  _const_spec(e.shape), _const_spec(wa.shape), _const_spec(wb.shape), vec,
                  vec, vec, _const_spec(wq.shape), _const_spec(wo.shape), kv, kv],
        out_specs=row(d),
        compiler_params=pltpu.CompilerParams(
            dimension_semantics=("arbitrary", "arbitrary"), vmem_limit_bytes=VMEM_LIMIT),
        name="mixout_xattn",
    )(x, oa, oc, osel, ow, gates, e, wa, wb, mix_post_g, pre_g, post_g, wq, wo, k, v)


def _memkv_kernel(m_ref, g_ref, wk_ref, wv_ref, k_ref, v_ref):
    m = _rms(m_ref[...], g_ref[...]).astype(BF)
    k_ref[...] = _dot(m, wk_ref[...]).astype(BF)
    v_ref[...] = _dot(m, wv_ref[...]).astype(BF)


def _memkv(mem, g, wk, wv):
    b, nm, d = mem.shape
    spec = pl.BlockSpec((None, nm, d), lambda i: (i, 0, 0))
    return pl.pallas_call(
        _memkv_kernel,
        out_shape=[jax.ShapeDtypeStruct((b, nm, d), BF)] * 2,
        grid=(b,),
        in_specs=[spec, _const_spec((1, d)), _const_spec(wk.shape), _const_spec(wv.shape)],
        out_specs=[spec, spec],
        compiler_params=pltpu.CompilerParams(
            dimension_semantics=("arbitrary",), vmem_limit_bytes=VMEM_LIMIT),
        name="mem_kv",
    )(mem, g, wk, wv)


def _bf16_terms(x, n):
    terms = []
    for _ in range(n):
        u = np.float32(x).view(np.uint32)
        hi = np.uint32((int(u) + 0x7FFF + ((int(u) >> 16) & 1)) & 0xFFFF0000).view(np.float32)
        terms.append(float(hi))
        x = float(np.float64(x) - np.float64(hi))
    return terms


def _query_feature_rows(slopes_by_chunk):
    out = np.zeros((1, len(slopes_by_chunk) * LANES), np.float32)
    for j, s in enumerate(slopes_by_chunk):
        base = j * LANES + (HALF if j % 2 == 0 else 0)
        c64 = _bf16_terms(s * 64.0 * LOG2E, POS_FEATS // 2)
        c1 = _bf16_terms(s * LOG2E, POS_FEATS // 2)
        for i in range(POS_FEATS // 2):
            out[0, base + 2 * i] = c64[i]
            out[0, base + 2 * i + 1] = c1[i]
    return jnp.asarray(out)


def _static_tables(t):
    ncp = t // CMP_STRIDE
    ns = t // SEL_LEN
    da = _alibi_slopes(DA_HEADS)
    qf_da = _query_feature_rows([da[j // 2] for j in range(2 * DA_HEADS)])
    nsa = _alibi_slopes(NSA_GROUPS * NSA_REP)
    qf_n = _query_feature_rows([nsa[(j % 2) * NSA_REP + j // 2] for j in range(2 * NSA_REP)])

    c_end = np.arange(ncp) * CMP_STRIDE + CMP_LEN - 1
    cpos = np.zeros((NSA_GROUPS, ncp, LANES), np.float32)
    for i in range(POS_FEATS // 2):
        cpos[0, :, HALF + 2 * i] = c_end // 64
        cpos[0, :, HALF + 2 * i + 1] = c_end % 64
        cpos[1, :, 2 * i] = c_end // 64
        cpos[1, :, 2 * i + 1] = c_end % 64

    c_start = np.arange(ncp) * CMP_STRIDE
    s_start = np.arange(ns) * SEL_LEN
    overlap = np.clip(np.minimum(c_start[:, None] + CMP_LEN, s_start[None, :] + SEL_LEN)
                      - np.maximum(c_start[:, None], s_start[None, :]), 0, None)
    mt = (overlap.astype(np.float32) / CMP_LEN).T
    mt[:, ncp - 1] = 0.0

    e = np.zeros((3, LANES, NSA_REP * LANES), np.float32)
    for g in range(NSA_GROUPS):
        for r in range(NSA_REP):
            for c in range(3):
                col = r * LANES + g * HALF
                e[c, g * NSA_REP * 3 + r * 3 + c, col:col + HALF] = 1.0
    e = np.concatenate([e, e], axis=1)
    return qf_da, qf_n, jnp.asarray(cpos), jnp.asarray(mt, BF), jnp.asarray(e, BF)


def _compress_weights(pe, w1, w2):
    hidden = w1.shape[1]
    w1r = w1.reshape(2, CMP_STRIDE, NSA_HEAD_DIM, hidden)
    w1g = jnp.zeros((2, NSA_GROUPS, CMP_STRIDE, NSA_GROUPS, NSA_HEAD_DIM, hidden), w1.dtype)
    for g in range(NSA_GROUPS):
        w1g = w1g.at[:, g, :, g].set(w1r)
    w1g = w1g.reshape(2, NSA_GROUPS, CMP_STRIDE * LANES, hidden).astype(BF)
    per = pe.reshape(2, CMP_STRIDE, 1, NSA_HEAD_DIM)
    peg = jnp.broadcast_to(per, (2, CMP_STRIDE, NSA_GROUPS, NSA_HEAD_DIM)).reshape(2, 1, CMP_STRIDE * LANES)
    w2g = jnp.zeros((NSA_GROUPS, hidden, NSA_GROUPS, NSA_HEAD_DIM), w2.dtype)
    for g in range(NSA_GROUPS):
        w2g = w2g.at[g, :, g].set(w2)
    w2g = w2g.reshape(NSA_GROUPS, hidden, LANES).astype(BF)
    return peg, w1g, w2g


def kernel(x, mem, ffn1_pre_g, ffn1_post_g, ffn1_w_gate, ffn1_w_up, ffn1_w_down, mix_pre_g, mix_post_g,
           w_mix_in, da_lambda_q1, da_lambda_k1, da_lambda_q2, da_lambda_k2, da_subln_g, cmp_k_pe,
           cmp_k_w1, cmp_k_w2, cmp_v_pe, cmp_v_w1, cmp_v_w2, w_mix_out, xa_pre_g, xa_post_g, mem_norm_g,
           xa_w_q, xa_w_k, xa_w_v, xa_w_o, ffn2_pre_g, ffn2_post_g, ffn2_w_gate, ffn2_w_up, ffn2_w_down):
    b, t, d = x.shape
    depth = ffn1_pre_g.shape[0]
    n = b * t
    tm = 512
    tq_da = 512
    tq = 256
    tq_cmp = min(1024, t)
    ns = t // SEL_LEN
    assert t % tm == 0 and ns <= LANES and WIN % tq == 0
    assert d == DA_HEADS * DA_V_DIM + NSA_GROUPS * NSA_REP * NSA_HEAD_DIM
    qf_da, qf_n, cpos, mt, e = _static_tables(t)
    row = lambda v: v.reshape(1, -1)

    for l in range(depth):
        x2 = _ffn(x.reshape(n, d), row(ffn1_pre_g[l]), row(ffn1_post_g[l]),
                  ffn1_w_gate[l].astype(BF), ffn1_w_up[l].astype(BF), ffn1_w_down[l].astype(BF), tm=tm)

        w = w_mix_in[l]
        dq = DA_HEADS * 2 * DA_QK_DIM
        dv = DA_HEADS * DA_V_DIM
        nqw = NSA_GROUPS * NSA_REP * NSA_HEAD_DIM
        nkv = NSA_GROUPS * NSA_HEAD_DIM
        o0 = 2 * dq + dv
        w_nq = w[:, o0:o0 + nqw].reshape(d, NSA_GROUPS, NSA_REP, NSA_HEAD_DIM).transpose(0, 2, 1, 3).reshape(d, nqw)
        o1 = o0 + nqw
        seg = lambda i: w[:, o1 + i * nkv:o1 + (i + 1) * nkv]
        w_g = w[:, o1 + 6 * nkv:]
        w_g = jnp.pad(w_g, ((0, 0), (0, LANES - w_g.shape[1])))
        w_big = jnp.concatenate([w[:, :o0], w_nq, seg(2), seg(4), seg(3), seg(5), seg(0), seg(1), w_g],
                                axis=1).astype(BF)
        (daq, dak, davt, nq, nks, nkw, nvst, nvwt, nkc, nvc, gates) = _mixin(
            x2.reshape(b, t, d), row(mix_pre_g[l]), w_big, qf_da, qf_n, tm=tm, tk_da=tq_da, tk_nsa=tq)

        lam_init = 0.8 - 0.6 * float(np.exp(-0.3 * l))
        o_a = _da_attention(daq, dak, davt, row(da_lambda_q1[l]), row(da_lambda_k1[l]),
                            row(da_lambda_q2[l]), row(da_lambda_k2[l]), row(da_subln_g[l]),
                            lam_init=lam_init, tq=tq_da)

        pek, w1k, w2k = _compress_weights(cmp_k_pe[l], cmp_k_w1[l], cmp_k_w2[l])
        pev, w1v, w2v = _compress_weights(cmp_v_pe[l], cmp_v_w1[l], cmp_v_w2[l])
        ncp = t // CMP_STRIDE
        kc, vct = _compress(nkc.reshape(b, ncp, CMP_STRIDE * LANES), nvc.reshape(b, ncp, CMP_STRIDE * LANES),
                            pek, pev, w1k, w1v, w2k, w2v, cpos)
        o_cmp, sf = _cmp_topk(nq, kc, vct, mt, tq=tq_cmp, topk=min(SEL_TOPK, ns))
        o_sel = _nsa_flash(nq, nks, nvst, sf, tq=tq, window=False)
        o_win = _nsa_flash(nq, nkw, nvwt, tq=tq, window=True)

        wo = w_mix_out[l]
        wa = wo[:dv].astype(BF)
        wb = wo[dv:].reshape(NSA_GROUPS, NSA_REP, NSA_HEAD_DIM, d).transpose(1, 0, 2, 3).reshape(nqw, d).astype(BF)
        mk, mv = _memkv(mem, row(mem_norm_g[l]), xa_w_k[l].astype(BF), xa_w_v[l].astype(BF))
        x4 = _mixout_xa(x2.reshape(b, t, d), o_a, o_cmp, o_sel, o_win, gates, e, wa, wb, row(mix_post_g[l]),
                        row(xa_pre_g[l]), row(xa_post_g[l]), xa_w_q[l].astype(BF), xa_w_o[l].astype(BF),
                        mk, mv, tm=tm)

        x = _ffn(x4.reshape(n, d), row(ffn2_pre_g[l]), row(ffn2_post_g[l]),
                 ffn2_w_gate[l].astype(BF), ffn2_w_up[l].astype(BF), ffn2_w_down[l].astype(BF),
                 tm=tm).reshape(b, t, d)
    return x
```

```python
import functools

import numpy as np
import jax
import jax.numpy as jnp
from jax import lax
from jax.experimental import pallas as pl
from jax.experimental.pallas import tpu as pltpu

BF = jnp.bfloat16
F32 = jnp.float32

EPS = 1e-6
NEG = -1e30
LANES = 128
HALF = LANES // 2
BF16_ROWS = 16

DA_HEADS = 4
DA_QK_DIM = 64
DA_V_DIM = 128
NSA_GROUPS = 2
NSA_REP = 4
NSA_HEAD_DIM = 64
CMP_LEN = 32
CMP_STRIDE = 16
SEL_LEN = 64
SEL_TOPK = 16
WIN = 512
FORCE_SCORE = 1e4
XA_HEADS = 4
SEL_MASK_BIG = 32768.0
LOG2E = 1.4426950408889634
QK_SCALE_LOG2 = 0.125 * LOG2E
POS_FEATS = 6

VMEM_LIMIT = 56 * 1024 * 1024

_NT = (((1,), (1,)), ((), ()))


def _rms(x, g):
    ms = jnp.mean(x * x, axis=-1, keepdims=True)
    return x * lax.rsqrt(ms + EPS) * g


def _dot(a, b):
    return jnp.dot(a, b, preferred_element_type=F32)


def _dot_nt(a, b):
    return lax.dot_general(a, b, _NT, preferred_element_type=F32)


def _const_spec(shape):
    nd = len(shape)
    return pl.BlockSpec(shape, lambda *_: (0,) * nd, pipeline_mode=pl.Buffered(1))


def _alibi_slopes(n):
    return [2.0 ** (-8.0 * (i + 1) / n) for i in range(n)]


def _softmax_step(s, m_old, l_old, acc_old, vt, ones):
    m_new = jnp.maximum(m_old, jnp.max(s, axis=0, keepdims=True))
    alpha = jnp.exp2(m_old - m_new)
    p = jnp.exp2(s - m_new).astype(BF)
    l_new = alpha * l_old + _dot(ones, p)[0:1]
    acc_new = alpha * acc_old + _dot(vt, p)
    return m_new, l_new, acc_new


def _ffn_kernel(x_ref, pre_ref, post_ref, wg_ref, wu_ref, wd_ref, o_ref):
    x = x_ref[...]
    h = _rms(x, pre_ref[...]).astype(BF)
    g = _dot(h, wg_ref[...])
    u = _dot(h, wu_ref[...])
    a = (g * jax.nn.sigmoid(g) * u).astype(BF)
    y = _dot(a, wd_ref[...])
    o_ref[...] = x + 0.5 * _rms(y, post_ref[...])


def _ffn(x2, pre_g, post_g, wg, wu, wd, *, tm):
    n, d = x2.shape
    f = wg.shape[1]
    return pl.pallas_call(
        _ffn_kernel,
        out_shape=jax.ShapeDtypeStruct((n, d), F32),
        grid=(n // tm,),
        in_specs=[
            pl.BlockSpec((tm, d), lambda i: (i, 0)),
            _const_spec((1, d)), _const_spec((1, d)),
            _const_spec((d, f)), _const_spec((d, f)), _const_spec((f, d)),
        ],
        out_specs=pl.BlockSpec((tm, d), lambda i: (i, 0)),
        compiler_params=pltpu.CompilerParams(
            dimension_semantics=("arbitrary",), vmem_limit_bytes=VMEM_LIMIT),
        name="ffn",
    )(x2, pre_g, post_g, wg, wu, wd)


def _mixin_kernel(x_ref, g_ref, w_ref, qfda_ref, qfn_ref,
                  daq, dak, davt, nq, nks, nkw, nvst, nvwt, nkc, nvc, gat, *, tk_da, tk_nsa):
    tm = x_ref.shape[0]
    t0 = pl.program_id(1) * tm
    h = _rms(x_ref[...], g_ref[...]).astype(BF)
    z = _dot(h, w_ref[...])

    lane = lax.broadcasted_iota(jnp.int32, (tm, LANES), 1)
    row = lax.broadcasted_iota(jnp.int32, (tm, LANES), 0) + t0
    pa = (row >> 6).astype(F32)
    pb = (row & 63).astype(F32)
    low = lane < HALF
    pab = jnp.where((lane & 1) == 0, pa, pb)
    pos_lo = jnp.where(lane < HALF, 0.0, jnp.where(lane < HALF + POS_FEATS, pab, 0.0))
    pos_hi = jnp.where(lane < POS_FEATS, pab, 0.0)
    blk_onehot = jnp.where(lane == (row >> 6), 1.0, 0.0)

    def ch(i):
        return z[:, i * LANES:(i + 1) * LANES]

    def put(ref, j, val):
        ref[:, j * LANES:(j + 1) * LANES] = val.astype(ref.dtype)

    for c in range(4):
        q = ch(c) * QK_SCALE_LOG2
        put(daq, 2 * c, jnp.where(low, q, qfda_ref[:, (2 * c) * LANES:(2 * c + 1) * LANES]))
        put(daq, 2 * c + 1, jnp.where(low, qfda_ref[:, (2 * c + 1) * LANES:(2 * c + 2) * LANES], q))
        k = ch(4 + c)
        put(dak, 2 * c, jnp.where(low, k, pos_lo))
        put(dak, 2 * c + 1, jnp.where(low, pos_hi, k))
        vt = ch(8 + c).T
        for s in range(tm // tk_da):
            davt[s, c * LANES:(c + 1) * LANES, :] = vt[:, s * tk_da:(s + 1) * tk_da].astype(BF)
        q = ch(12 + c) * QK_SCALE_LOG2
        put(nq, 2 * c, jnp.where(low, q, qfn_ref[:, (2 * c) * LANES:(2 * c + 1) * LANES]))
        put(nq, 2 * c + 1, jnp.where(low, qfn_ref[:, (2 * c + 1) * LANES:(2 * c + 2) * LANES], q))
    k = ch(16)
    put(nks, 0, jnp.where(low, k, pos_lo))
    put(nks, 1, blk_onehot)
    put(nks, 2, jnp.where(low, pos_hi, k))
    put(nks, 3, blk_onehot)
    k = ch(17)
    put(nkw, 0, jnp.where(low, k, pos_lo))
    put(nkw, 1, jnp.where(low, pos_hi, k))
    for src, ref in ((18, nvst), (19, nvwt)):
        vt = ch(src).T
        for s in range(tm // tk_nsa):
            ref[s] = vt[:, s * tk_nsa:(s + 1) * tk_nsa].astype(BF)
    nkc[...] = ch(20)
    nvc[...] = ch(21)
    gat[...] = jax.nn.sigmoid(ch(22))


def _mixin(x, g, w, qf_da, qf_n, *, tm, tk_da, tk_nsa):
    b, t, d = x.shape
    wcols = w.shape[1]
    assert tm % tk_da == 0 and tm % tk_nsa == 0
    row = lambda wd, dt: (jax.ShapeDtypeStruct((b, t, wd), dt),
                          pl.BlockSpec((None, tm, wd), lambda i, j: (i, j, 0)))
    tr = lambda rows, tk: (jax.ShapeDtypeStruct((b, t // tk, rows, tk), BF),
                           pl.BlockSpec((None, tm // tk, rows, tk), lambda i, j: (i, j, 0, 0)))
    outs = [row(1024, BF), row(1024, BF), tr(512, tk_da), row(1024, BF), row(512, BF), row(256, BF),
            tr(LANES, tk_nsa), tr(LANES, tk_nsa), row(LANES, F32), row(LANES, F32), row(LANES, F32)]
    return pl.pallas_call(
        functools.partial(_mixin_kernel, tk_da=tk_da, tk_nsa=tk_nsa),
        out_shape=[o[0] for o in outs],
        grid=(b, t // tm),
        in_specs=[
            pl.BlockSpec((None, tm, d), lambda i, j: (i, j, 0)),
            _const_spec((1, d)), _const_spec((d, wcols)),
            _const_spec((1, 1024)), _const_spec((1, 1024)),
        ],
        out_specs=[o[1] for o in outs],
        compiler_params=pltpu.CompilerParams(
            dimension_semantics=("arbitrary", "arbitrary"), vmem_limit_bytes=VMEM_LIMIT),
        name="mixin",
    )(x, g, w, qf_da, qf_n)


def _da_kernel(q_ref, k_ref, vt_ref, lq1, lk1, lq2, lk2, sg_ref, o_ref, s_ref, acc_ref, m_ref, l_ref,
               *, lam_init, tq, strip):
    qi = pl.program_id(2)
    m_ref[...] = jnp.full(m_ref.shape, NEG, F32)
    l_ref[...] = jnp.zeros(l_ref.shape, F32)
    acc_ref[...] = jnp.zeros(acc_ref.shape, F32)

    def scores(j, buf):
        ks = pl.multiple_of(j * tq, tq)
        for mp in range(2):
            lanes = slice(mp * LANES, (mp + 1) * LANES)
            s_ref[buf, mp] = _dot_nt(k_ref[pl.ds(ks, tq), lanes], q_ref[:, lanes])

    def consume(j, buf, diag=False):
        for mp in range(2):
            for c in range(tq // strip):
                sl = slice(c * strip, (c + 1) * strip)
                nk = (c + 1) * strip if diag else tq
                s = s_ref[buf, mp, 0:nk, sl]
                if diag:
                    kk = lax.broadcasted_iota(jnp.int32, (nk, strip), 0)
                    qq = lax.broadcasted_iota(jnp.int32, (nk, strip), 1) + c * strip
                    s = jnp.where(kk <= qq, s, NEG)
                m_ref[mp, :, sl], l_ref[mp, :, sl], acc_ref[mp, :, sl] = _softmax_step(
                    s, m_ref[mp, :, sl], l_ref[mp, :, sl], acc_ref[mp, :, sl],
                    vt_ref[j, :, 0:nk], jnp.ones((BF16_ROWS, nk), BF))

    scores(0, 0)

    def pair(p, carry):
        scores(2 * p + 1, 1)
        consume(2 * p, 0)
        scores(2 * p + 2, 0)
        consume(2 * p + 1, 1)
        return carry

    lax.fori_loop(0, qi // 2, pair, 0)

    @pl.when(qi % 2 == 1)
    def _():
        scores(qi, 1)
        consume(qi - 1, 0)
        consume(qi, 1, diag=True)

    @pl.when(qi % 2 == 0)
    def _():
        consume(qi, 0, diag=True)

    lam = (jnp.exp(jnp.sum(lq1[...] * lk1[...], axis=1, keepdims=True))
           - jnp.exp(jnp.sum(lq2[...] * lk2[...], axis=1, keepdims=True)) + lam_init)
    o = (acc_ref[0] / l_ref[0] - lam * (acc_ref[1] / l_ref[1])).T
    o_ref[...] = (_rms(o, sg_ref[...]) * (1.0 - lam_init)).astype(o_ref.dtype)


def _da_attention(daq, dak, davt, lq1, lk1, lq2, lk2, subln_g, *, lam_init, tq):
    b, t, _ = daq.shape
    nk = t // tq
    vec = _const_spec((1, DA_QK_DIM))
    return pl.pallas_call(
        functools.partial(_da_kernel, lam_init=lam_init, tq=tq, strip=2 * LANES),
        out_shape=jax.ShapeDtypeStruct((b, t, DA_HEADS * DA_V_DIM), BF),
        grid=(b, DA_HEADS, t // tq),
        in_specs=[
            pl.BlockSpec((None, tq, 2 * LANES), lambda i, h, j: (i, j, h)),
            pl.BlockSpec((None, t, 2 * LANES), lambda i, h, j: (i, 0, h)),
            pl.BlockSpec((None, nk, DA_V_DIM, tq), lambda i, h, j: (i, 0, h, 0)),
            vec, vec, vec, vec, _const_spec((1, DA_V_DIM)),
        ],
        out_specs=pl.BlockSpec((None, tq, DA_V_DIM), lambda i, h, j: (i, j, h)),
        scratch_shapes=[
            pltpu.VMEM((2, 2, tq, tq), F32),
            pltpu.VMEM((2, DA_V_DIM, tq), F32),
            pltpu.VMEM((2, 1, tq), F32),
            pltpu.VMEM((2, 1, tq), F32),
        ],
        compiler_params=pltpu.CompilerParams(
            dimension_semantics=("arbitrary", "arbitrary", "arbitrary"), vmem_limit_bytes=VMEM_LIMIT),
        name="diff_attn",
    )(daq, dak, davt, lq1, lk1, lq2, lk2, subln_g)


def _compress_kernel(xk_ref, xv_ref, pek_ref, pev_ref, w1k_ref, w1v_ref, w2k_ref, w2v_ref, cpos_ref,
                     kc_ref, vct_ref):
    ncp = xk_ref.shape[0]
    for x_ref, pe_ref, w1_ref, w2_ref, is_v in ((xk_ref, pek_ref, w1k_ref, w2k_ref, False),
                                                (xv_ref, pev_ref, w1v_ref, w2v_ref, True)):
        x = x_ref[...]
        xa = (x + pe_ref[0]).astype(BF)
        xb = (x + pe_ref[1]).astype(BF)
        for g in range(NSA_GROUPS):
            first = _dot(xa, w1_ref[0, g])
            second = _dot(xb, w1_ref[1, g])
            hid = first + pltpu.roll(second, ncp - 1, 0)
            act = (hid * jax.nn.sigmoid(hid)).astype(BF)
            out = _dot(act, w2_ref[g])
            if is_v:
                vct_ref[g] = out.T.astype(BF)
            else:
                kc_ref[g] = (out + cpos_ref[g]).astype(BF)


def _compress(xk, xv, pek, pev, w1k, w1v, w2k, w2v, cpos):
    b, ncp, width = xk.shape
    x_spec = pl.BlockSpec((None, ncp, width), lambda i: (i, 0, 0))
    return pl.pallas_call(
        _compress_kernel,
        out_shape=[jax.ShapeDtypeStruct((b, NSA_GROUPS, ncp, LANES), BF),
                   jax.ShapeDtypeStruct((b, NSA_GROUPS, LANES, ncp), BF)],
        grid=(b,),
        in_specs=[x_spec, x_spec,
                  _const_spec(pek.shape), _const_spec(pev.shape),
                  _const_spec(w1k.shape), _const_spec(w1v.shape),
                  _const_spec(w2k.shape), _const_spec(w2v.shape), _const_spec(cpos.shape)],
        out_specs=[pl.BlockSpec((None, NSA_GROUPS, ncp, LANES), lambda i: (i, 0, 0, 0)),
                   pl.BlockSpec((None, NSA_GROUPS, LANES, ncp), lambda i: (i, 0, 0, 0))],
        compiler_params=pltpu.CompilerParams(
            dimension_semantics=("arbitrary",), vmem_limit_bytes=VMEM_LIMIT),
        name="compress",
    )(xk, xv, pek, pev, w1k, w1v, w2k, w2v, cpos)


def _cmp_kernel(nq_ref, kc_ref, vct_ref, mt_ref, o_ref, sf_ref, ob_ref, *, tq, topk):
    ncp = kc_ref.shape[1]
    ns = mt_ref.shape[0]
    t0 = pl.program_id(1) * tq
    cidx = lax.broadcasted_iota(jnp.int32, (ncp, tq), 0)
    tpos = lax.broadcasted_iota(jnp.int32, (ncp, tq), 1) + t0
    valid = (cidx * CMP_STRIDE + (CMP_LEN - 1)) <= tpos
    col_live = (lax.broadcasted_iota(jnp.int32, (1, tq), 1) + t0) >= CMP_LEN - 1
    lane = lax.broadcasted_iota(jnp.int32, (tq, LANES), 1)
    mt = mt_ref[...]

    blk = lax.broadcasted_iota(jnp.int32, (ns, tq), 0)
    cur = (lax.broadcasted_iota(jnp.int32, (ns, tq), 1) + t0) >> 6
    sub8 = lax.broadcasted_iota(jnp.int32, (8, tq), 0)

    for g in range(NSA_GROUPS):
        kc = kc_ref[g]
        vct = vct_ref[g]
        imp = jnp.zeros((ns, tq), F32)
        for r in range(NSA_REP):
            c = 2 * r + g
            q = nq_ref[:, c * LANES:(c + 1) * LANES]
            s = jnp.where(valid, _dot_nt(kc, q), NEG)
            m = jnp.max(s, axis=0, keepdims=True)
            e = jnp.exp2(s - m)
            p = e * jnp.where(col_live, 1.0 / jnp.sum(e, axis=0, keepdims=True), 0.0)
            p_hi = p.astype(BF)
            imp = imp + _dot(mt, p_hi)
            o = _dot(vct, p_hi).T
            if g == 0:
                ob_ref[r] = o
            else:
                o_ref[:, r * LANES:(r + 1) * LANES] = jnp.where(lane < HALF, ob_ref[r], o).astype(o_ref.dtype)

        score = jnp.where(blk == 0, FORCE_SCORE,
                          jnp.where(blk == cur, FORCE_SCORE,
                                    jnp.where(blk == cur - 1, FORCE_SCORE,
                                              jnp.where(blk <= cur, imp, -1.0))))
        rows = [jnp.broadcast_to(score[i:i + 1, :], (8, tq)) for i in range(ns)]
        feats = []
        for kt in range(ns // 8):
            sc = score[8 * kt:8 * kt + 8, :]
            cnt = jnp.zeros((8, tq), F32)
            for i in range(ns):
                if i < 8 * kt:
                    beat = jnp.where(rows[i] >= sc, 1.0, 0.0)
                elif i >= 8 * kt + 8:
                    beat = jnp.where(rows[i] > sc, 1.0, 0.0)
                else:
                    beat = jnp.where(sub8 + 8 * kt > i,
                                     jnp.where(rows[i] >= sc, 1.0, 0.0),
                                     jnp.where(rows[i] > sc, 1.0, 0.0))
                cnt = cnt + beat
            feats.append(jnp.where(cnt < topk, 0.0, -SEL_MASK_BIG))
        feats.append(jnp.zeros((LANES - ns, tq), F32))
        sf_ref[g] = jnp.concatenate(feats, axis=0).T.astype(BF)


def _cmp_topk(nq, kc, vct, mt, *, tq, topk):
    b, t, _ = nq.shape
    ncp = kc.shape[2]
    return pl.pallas_call(
        functools.partial(_cmp_kernel, tq=tq, topk=topk),
        out_shape=[jax.ShapeDtypeStruct((b, t, NSA_REP * LANES), BF),
                   jax.ShapeDtypeStruct((b, NSA_GROUPS, t, LANES), BF)],
        grid=(b, t // tq),
        in_specs=[
            pl.BlockSpec((None, tq, 2 * NSA_REP * LANES), lambda i, j: (i, j, 0)),
            pl.BlockSpec((None, NSA_GROUPS, ncp, LANES), lambda i, j: (i, 0, 0, 0)),
            pl.BlockSpec((None, NSA_GROUPS, LANES, ncp), lambda i, j: (i, 0, 0, 0)),
            _const_spec(mt.shape),
        ],
        out_specs=[pl.BlockSpec((None, tq, NSA_REP * LANES), lambda i, j: (i, j, 0)),
                   pl.BlockSpec((None, NSA_GROUPS, tq, LANES), lambda i, j: (i, 0, j, 0))],
        scratch_shapes=[pltpu.VMEM((NSA_REP, tq, LANES), F32)],
        compiler_params=pltpu.CompilerParams(
            dimension_semantics=("arbitrary", "arbitrary"), vmem_limit_bytes=VMEM_LIMIT),
        name="cmp_topk",
    )(nq, kc, vct, mt)


def _nsa_flash_kernel(*refs, tq, window, selected):
    if selected:
        nq_ref, sf_ref, k_ref, vt_ref, o_ref, qs_ref, s_ref, acc_ref, m_ref, l_ref, ob_ref = refs
    else:
        nq_ref, k_ref, vt_ref, o_ref, qs_ref, s_ref, acc_ref, m_ref, l_ref, ob_ref = refs
    qi = pl.program_id(1)
    kd = qs_ref.shape[2]
    cols = NSA_REP * tq
    key = lax.broadcasted_iota(jnp.int32, (tq, tq), 0)
    qry = lax.broadcasted_iota(jnp.int32, (tq, tq), 1)
    ones = jnp.ones((BF16_ROWS, tq), BF)

    for g in range(NSA_GROUPS):
        for r in range(NSA_REP):
            c = 2 * r + g
            qs_ref[g, r * tq:(r + 1) * tq, 0:LANES] = nq_ref[:, c * LANES:(c + 1) * LANES]
            if selected:
                qs_ref[g, r * tq:(r + 1) * tq, LANES:2 * LANES] = sf_ref[g]

    m_ref[...] = jnp.full(m_ref.shape, NEG, F32)
    l_ref[...] = jnp.zeros(l_ref.shape, F32)
    acc_ref[...] = jnp.zeros(acc_ref.shape, F32)

    def scores(j, buf):
        ks = pl.multiple_of(j * tq, tq)
        for g in range(NSA_GROUPS):
            s_ref[buf, g] = _dot_nt(k_ref[pl.ds(ks, tq), g * kd:(g + 1) * kd], qs_ref[g])

    def consume(j, buf, mode, live=None):
        if mode == "edge":
            edge_qry = jnp.where(live, qry, tq)
        for g in range(NSA_GROUPS):
            vt = vt_ref[j, g * NSA_HEAD_DIM:(g + 1) * NSA_HEAD_DIM, :]
            for r in range(NSA_REP):
                sl = slice(r * tq, (r + 1) * tq)
                s = s_ref[buf, g, :, sl]
                if mode == "diag":
                    s = jnp.where(key <= qry, s, NEG)
                elif mode == "edge":
                    s = jnp.where(key > edge_qry, s, NEG)
                elif live is not None:
                    s = jnp.where(live, s, NEG)
                m_ref[g, :, sl], l_ref[g, :, sl], acc_ref[g, :, sl] = _softmax_step(
                    s, m_ref[g, :, sl], l_ref[g, :, sl], acc_ref[g, :, sl], vt, ones)

    if window:
        nback = WIN // tq
        for i in range(nback + 1):
            j = qi - nback + i
            scores(jnp.maximum(j, 0), i)
            if i == nback:
                consume(j, i, "diag")
            else:
                consume(jnp.maximum(j, 0), i, "edge" if i == 0 else "full", j >= 0)
    else:
        scores(0, 0)

        def pair(p, carry):
            scores(2 * p + 1, 1)
            consume(2 * p, 0, "full")
            scores(2 * p + 2, 0)
            consume(2 * p + 1, 1, "full")
            return carry

        lax.fori_loop(0, qi // 2, pair, 0)

        @pl.when(qi % 2 == 1)
        def _():
            scores(qi, 1)
            consume(qi - 1, 0, "full")
            consume(qi, 1, "diag")

        @pl.when(qi % 2 == 0)
        def _():
            consume(qi, 0, "diag")

    for g in range(NSA_GROUPS):
        o = acc_ref[g] / l_ref[g]
        for r in range(NSA_REP):
            ob_ref[r, g * NSA_HEAD_DIM:(g + 1) * NSA_HEAD_DIM, :] = o[:, r * tq:(r + 1) * tq]

    for r in range(NSA_REP):
        o_ref[:, r * LANES:(r + 1) * LANES] = ob_ref[r].T.astype(o_ref.dtype)


def _nsa_flash(nq, k, vt, sf=None, *, tq, window):
    b, t, _ = nq.shape
    selected = sf is not None
    kd = 2 * LANES if selected else LANES
    nk = t // tq
    assert vt.shape == (b, nk, LANES, tq) and k.shape == (b, t, NSA_GROUPS * kd)
    in_specs = [pl.BlockSpec((None, tq, 2 * NSA_REP * LANES), lambda i, j: (i, j, 0))]
    args = [nq]
    if selected:
        in_specs.append(pl.BlockSpec((None, NSA_GROUPS, tq, LANES), lambda i, j: (i, 0, j, 0)))
        args.append(sf)
    in_specs.append(pl.BlockSpec((None, t, NSA_GROUPS * kd), lambda i, j: (i, 0, 0)))
    args.append(k)
    in_specs.append(pl.BlockSpec((None, nk, LANES, tq), lambda i, j: (i, 0, 0, 0)))
    args.append(vt)
    cols = NSA_REP * tq
    return pl.pallas_call(
        functools.partial(_nsa_flash_kernel, tq=tq, window=window, selected=selected),
        out_shape=jax.ShapeDtypeStruct((b, t, NSA_REP * LANES), BF),
        grid=(b, t // tq),
        in_specs=in_specs,
        out_specs=pl.BlockSpec((None, tq, NSA_REP * LANES), lambda i, j: (i, j, 0)),
        scratch_shapes=[
            pltpu.VMEM((NSA_GROUPS, cols, kd), BF),
            pltpu.VMEM((WIN // tq + 1 if window else 2, NSA_GROUPS, tq, cols), F32),
            pltpu.VMEM((NSA_GROUPS, NSA_HEAD_DIM, cols), F32),
            pltpu.VMEM((NSA_GROUPS, 1, cols), F32),
            pltpu.VMEM((NSA_GROUPS, 1, cols), F32),
            pltpu.VMEM((NSA_REP, LANES, tq), F32),
        ],
        compiler_params=pltpu.CompilerParams(
            dimension_semantics=("arbitrary", "arbitrary"), vmem_limit_bytes=VMEM_LIMIT),
        name="nsa_window" if window else "nsa_selected",
    )(*args)


def _mixout_xa_kernel(x_ref, oa_ref, oc_ref, os_ref, ow_ref, gat_ref, e_ref, wa_ref, wb_ref, mpost_ref,
                      pre_ref, post_ref, wq_ref, wo_ref, k_ref, v_ref, o_ref, *, scale):
    gt = gat_ref[...]
    g_hi = gt.astype(BF)
    g_lo = (gt - g_hi.astype(F32)).astype(BF)
    g_split = jnp.concatenate([g_hi, g_lo], axis=1)
    ob = None
    for c, ref in enumerate((oc_ref, os_ref, ow_ref)):
        gate = _dot(g_split, e_ref[c])
        term = gate * ref[...].astype(F32)
        ob = term if ob is None else ob + term
    y = _dot(oa_ref[...], wa_ref[...]) + _dot(ob.astype(BF), wb_ref[...])
    x = x_ref[...] + _rms(y, mpost_ref[...])

    h = _rms(x, pre_ref[...]).astype(BF)
    q = (_dot(h, wq_ref[...]) * scale).astype(BF)
    hd = q.shape[1] // XA_HEADS
    outs = []
    for i in range(XA_HEADS):
        sl = slice(i * hd, (i + 1) * hd)
        s = _dot_nt(q[:, sl], k_ref[:, sl])
        m = jnp.max(s, axis=1, keepdims=True)
        e = jnp.exp(s - m)
        p = e / jnp.sum(e, axis=1, keepdims=True)
        outs.append(_dot(p.astype(BF), v_ref[:, sl]).astype(BF))
    y = _dot(jnp.concatenate(outs, axis=1), wo_ref[...])
    o_ref[...] = x + _rms(y, post_ref[...])


def _mixout_xa(x, oa, oc, osel, ow, gates, e, wa, wb, mix_post_g, pre_g, post_g, wq, wo, k, v, *, tm):
    b, t, d = x.shape
    nm = k.shape[1]
    hd = wq.shape[1] // XA_HEADS
    row = lambda w: pl.BlockSpec((None, tm, w), lambda i, j: (i, j, 0))
    kv = pl.BlockSpec((None, nm, k.shape[2]), lambda i, j: (i, 0, 0))
    vec = _const_spec((1, d))
    return pl.pallas_call(
        functools.partial(_mixout_xa_kernel, scale=float(hd) ** -0.5),
        out_shape=jax.ShapeDtypeStruct((b, t, d), F32),
        grid=(b, t // tm),
        in_specs=[row(d), row(oa.shape[2]), row(oc.shape[2]), row(osel.shape[2]), row(ow.shape[2]), row(LANES),
                  _const_spec(e.shape), _const_spec(wa.shape), _const_spec(wb.shape), vec,
                  vec, vec, _const_spec(wq.shape), _const_spec(wo.shape), kv, kv],
        out_specs=row(d),
        compiler_params=pltpu.CompilerParams(
            dimension_semantics=("arbitrary", "arbitrary"), vmem_limit_bytes=VMEM_LIMIT),
        name="mixout_xattn",
    )(x, oa, oc, osel, ow, gates, e, wa, wb, mix_post_g, pre_g, post_g, wq, wo, k, v)


def _memkv_kernel(m_ref, g_ref, wk_ref, wv_ref, k_ref, v_ref):
    m = _rms(m_ref[...], g_ref[...]).astype(BF)
    k_ref[...] = _dot(m, wk_ref[...]).astype(BF)
    v_ref[...] = _dot(m, wv_ref[...]).astype(BF)


def _memkv(mem, g, wk, wv):
    b, nm, d = mem.shape
    spec = pl.BlockSpec((None, nm, d), lambda i: (i, 0, 0))
    return pl.pallas_call(
        _memkv_kernel,
        out_shape=[jax.ShapeDtypeStruct((b, nm, d), BF)] * 2,
        grid=(b,),
        in_specs=[spec, _const_spec((1, d)), _const_spec(wk.shape), _const_spec(wv.shape)],
        out_specs=[spec, spec],
        compiler_params=pltpu.CompilerParams(
            dimension_semantics=("arbitrary",), vmem_limit_bytes=VMEM_LIMIT),
        name="mem_kv",
    )(mem, g, wk, wv)


def _bf16_terms(x, n):
    terms = []
    for _ in range(n):
        u = np.float32(x).view(np.uint32)
        hi = np.uint32((int(u) + 0x7FFF + ((int(u) >> 16) & 1)) & 0xFFFF0000).view(np.float32)
        terms.append(float(hi))
        x = float(np.float64(x) - np.float64(hi))
    return terms


def _query_feature_rows(slopes_by_chunk):
    out = np.zeros((1, len(slopes_by_chunk) * LANES), np.float32)
    for j, s in enumerate(slopes_by_chunk):
        base = j * LANES + (HALF if j % 2 == 0 else 0)
        c64 = _bf16_terms(s * 64.0 * LOG2E, POS_FEATS // 2)
        c1 = _bf16_terms(s * LOG2E, POS_FEATS // 2)
        for i in range(POS_FEATS // 2):
            out[0, base + 2 * i] = c64[i]
            out[0, base + 2 * i + 1] = c1[i]
    return jnp.asarray(out)


def _static_tables(t):
    ncp = t // CMP_STRIDE
    ns = t // SEL_LEN
    da = _alibi_slopes(DA_HEADS)
    qf_da = _query_feature_rows([da[j // 2] for j in range(2 * DA_HEADS)])
    nsa = _alibi_slopes(NSA_GROUPS * NSA_REP)
    qf_n = _query_feature_rows([nsa[(j % 2) * NSA_REP + j // 2] for j in range(2 * NSA_REP)])

    c_end = np.arange(ncp) * CMP_STRIDE + CMP_LEN - 1
    cpos = np.zeros((NSA_GROUPS, ncp, LANES), np.float32)
    for i in range(POS_FEATS // 2):
        cpos[0, :, HALF + 2 * i] = c_end // 64
        cpos[0, :, HALF + 2 * i + 1] = c_end % 64
        cpos[1, :, 2 * i] = c_end // 64
        cpos[1, :, 2 * i + 1] = c_end % 64

    c_start = np.arange(ncp) * CMP_STRIDE
    s_start = np.arange(ns) * SEL_LEN
    overlap = np.clip(np.minimum(c_start[:, None] + CMP_LEN, s_start[None, :] + SEL_LEN)
                      - np.maximum(c_start[:, None], s_start[None, :]), 0, None)
    mt = (overlap.astype(np.float32) / CMP_LEN).T
    mt[:, ncp - 1] = 0.0

    e = np.zeros((3, LANES, NSA_REP * LANES), np.float32)
    for g in range(NSA_GROUPS):
        for r in range(NSA_REP):
            for c in range(3):
                col = r * LANES + g * HALF
                e[c, g * NSA_REP * 3 + r * 3 + c, col:col + HALF] = 1.0
    e = np.concatenate([e, e], axis=1)
    return qf_da, qf_n, jnp.asarray(cpos), jnp.asarray(mt, BF), jnp.asarray(e, BF)


def _compress_weights(pe, w1, w2):
    hidden = w1.shape[1]
    w1r = w1.reshape(2, CMP_STRIDE, NSA_HEAD_DIM, hidden)
    w1g = jnp.zeros((2, NSA_GROUPS, CMP_STRIDE, NSA_GROUPS, NSA_HEAD_DIM, hidden), w1.dtype)
    for g in range(NSA_GROUPS):
        w1g = w1g.at[:, g, :, g].set(w1r)
    w1g = w1g.reshape(2, NSA_GROUPS, CMP_STRIDE * LANES, hidden).astype(BF)
    per = pe.reshape(2, CMP_STRIDE, 1, NSA_HEAD_DIM)
    peg = jnp.broadcast_to(per, (2, CMP_STRIDE, NSA_GROUPS, NSA_HEAD_DIM)).reshape(2, 1, CMP_STRIDE * LANES)
    w2g = jnp.zeros((NSA_GROUPS, hidden, NSA_GROUPS, NSA_HEAD_DIM), w2.dtype)
    for g in range(NSA_GROUPS):
        w2g = w2g.at[g, :, g].set(w2)
    w2g = w2g.reshape(NSA_GROUPS, hidden, LANES).astype(BF)
    return peg, w1g, w2g


def kernel(x, mem, ffn1_pre_g, ffn1_post_g, ffn1_w_gate, ffn1_w_up, ffn1_w_down, mix_pre_g, mix_post_g,
           w_mix_in, da_lambda_q1, da_lambda_k1, da_lambda_q2, da_lambda_k2, da_subln_g, cmp_k_pe,
           cmp_k_w1, cmp_k_w2, cmp_v_pe, cmp_v_w1, cmp_v_w2, w_mix_out, xa_pre_g, xa_post_g, mem_norm_g,
           xa_w_q, xa_w_k, xa_w_v, xa_w_o, ffn2_pre_g, ffn2_post_g, ffn2_w_gate, ffn2_w_up, ffn2_w_down):
    b, t, d = x.shape
    depth = ffn1_pre_g.shape[0]
    n = b * t
    tm = 512
    tq_da = 512
    tq = 256
    tq_cmp = min(1024, t)
    ns = t // SEL_LEN
    assert t % tm == 0 and ns <= LANES and WIN % tq == 0
    assert d == DA_HEADS * DA_V_DIM + NSA_GROUPS * NSA_REP * NSA_HEAD_DIM
    qf_da, qf_n, cpos, mt, e = _static_tables(t)
    row = lambda v: v.reshape(1, -1)

    for l in range(depth):
        x2 = _ffn(x.reshape(n, d), row(ffn1_pre_g[l]), row(ffn1_post_g[l]),
                  ffn1_w_gate[l].astype(BF), ffn1_w_up[l].astype(BF), ffn1_w_down[l].astype(BF), tm=tm)

        w = w_mix_in[l]
        dq = DA_HEADS * 2 * DA_QK_DIM
        dv = DA_HEADS * DA_V_DIM
        nqw = NSA_GROUPS * NSA_REP * NSA_HEAD_DIM
        nkv = NSA_GROUPS * NSA_HEAD_DIM
        o0 = 2 * dq + dv
        w_nq = w[:, o0:o0 + nqw].reshape(d, NSA_GROUPS, NSA_REP, NSA_HEAD_DIM).transpose(0, 2, 1, 3).reshape(d, nqw)
        o1 = o0 + nqw
        seg = lambda i: w[:, o1 + i * nkv:o1 + (i + 1) * nkv]
        w_g = w[:, o1 + 6 * nkv:]
        w_g = jnp.pad(w_g, ((0, 0), (0, LANES - w_g.shape[1])))
        w_big = jnp.concatenate([w[:, :o0], w_nq, seg(2), seg(4), seg(3), seg(5), seg(0), seg(1), w_g],
                                axis=1).astype(BF)
        (daq, dak, davt, nq, nks, nkw, nvst, nvwt, nkc, nvc, gates) = _mixin(
            x2.reshape(b, t, d), row(mix_pre_g[l]), w_big, qf_da, qf_n, tm=2 * tm, tk_da=tq_da, tk_nsa=tq)

        lam_init = 0.8 - 0.6 * float(np.exp(-0.3 * l))
        o_a = _da_attention(daq, dak, davt, row(da_lambda_q1[l]), row(da_lambda_k1[l]),
                            row(da_lambda_q2[l]), row(da_lambda_k2[l]), row(da_subln_g[l]),
                            lam_init=lam_init, tq=tq_da)

        pek, w1k, w2k = _compress_weights(cmp_k_pe[l], cmp_k_w1[l], cmp_k_w2[l])
        pev, w1v, w2v = _compress_weights(cmp_v_pe[l], cmp_v_w1[l], cmp_v_w2[l])
        ncp = t // CMP_STRIDE
        kc, vct = _compress(nkc.reshape(b, ncp, CMP_STRIDE * LANES), nvc.reshape(b, ncp, CMP_STRIDE * LANES),
                            pek, pev, w1k, w1v, w2k, w2v, cpos)
        o_cmp, sf = _cmp_topk(nq, kc, vct, mt, tq=tq_cmp, topk=min(SEL_TOPK, ns))
        o_sel = _nsa_flash(nq, nks, nvst, sf, tq=tq, window=False)
        o_win = _nsa_flash(nq, nkw, nvwt, tq=tq, window=True)

        wo = w_mix_out[l]
        wa = wo[:dv].astype(BF)
        wb = wo[dv:].reshape(NSA_GROUPS, NSA_REP, NSA_HEAD_DIM, d).transpose(1, 0, 2, 3).reshape(nqw, d).astype(BF)
        mk, mv = _memkv(mem, row(mem_norm_g[l]), xa_w_k[l].astype(BF), xa_w_v[l].astype(BF))
        x4 = _mixout_xa(x2.reshape(b, t, d), o_a, o_cmp, o_sel, o_win, gates, e, wa, wb, row(mix_post_g[l]),
                        row(xa_pre_g[l]), row(xa_post_g[l]), xa_w_q[l].astype(BF), xa_w_o[l].astype(BF),
                        mk, mv, tm=2 * tm)

        x = _ffn(x4.reshape(n, d), row(ffn2_pre_g[l]), row(ffn2_post_g[l]),
                 ffn2_w_gate[l].astype(BF), ffn2_w_up[l].astype(BF), ffn2_w_down[l].astype(BF),
                 tm=tm).reshape(b, t, d)
    return x
```

```python
import functools

import numpy as np
import jax
import jax.numpy as jnp
from jax import lax
from jax.experimental import pallas as pl
from jax.experimental.pallas import tpu as pltpu

BF = jnp.bfloat16
F32 = jnp.float32

EPS = 1e-6
NEG = -1e30
LANES = 128
HALF = LANES // 2
BF16_ROWS = 16

DA_HEADS = 4
DA_QK_DIM = 64
DA_V_DIM = 128
NSA_GROUPS = 2
NSA_REP = 4
NSA_HEAD_DIM = 64
CMP_LEN = 32
CMP_STRIDE = 16
SEL_LEN = 64
SEL_TOPK = 16
WIN = 512
FORCE_SCORE = 1e4
XA_HEADS = 4
SEL_MASK_BIG = 32768.0
LOG2E = 1.4426950408889634
QK_SCALE_LOG2 = 0.125 * LOG2E
POS_FEATS = 6

VMEM_LIMIT = 56 * 1024 * 1024

_NT = (((1,), (1,)), ((), ()))


def _rms(x, g):
    ms = jnp.mean(x * x, axis=-1, keepdims=True)
    return x * lax.rsqrt(ms + EPS) * g


def _dot(a, b):
    return jnp.dot(a, b, preferred_element_type=F32)


def _dot_nt(a, b):
    return lax.dot_general(a, b, _NT, preferred_element_type=F32)


def _const_spec(shape):
    nd = len(shape)
    return pl.BlockSpec(shape, lambda *_: (0,) * nd, pipeline_mode=pl.Buffered(1))


def _alibi_slopes(n):
    return [2.0 ** (-8.0 * (i + 1) / n) for i in range(n)]


def _softmax_step(s, m_old, l_old, acc_old, vt, ones):
    m_new = jnp.maximum(m_old, jnp.max(s, axis=0, keepdims=True))
    alpha = jnp.exp2(m_old - m_new)
    p = jnp.exp2(s - m_new).astype(BF)
    l_new = alpha * l_old + _dot(ones, p)[0:1]
    acc_new = alpha * acc_old + _dot(vt, p)
    return m_new, l_new, acc_new


def _ffn_kernel(x_ref, pre_ref, post_ref, wg_ref, wu_ref, wd_ref, o_ref):
    x = x_ref[...]
    h = _rms(x, pre_ref[...]).astype(BF)
    g = _dot(h, wg_ref[...])
    u = _dot(h, wu_ref[...])
    a = (g * jax.nn.sigmoid(g) * u).astype(BF)
    y = _dot(a, wd_ref[...])
    o_ref[...] = x + 0.5 * _rms(y, post_ref[...])


def _ffn(x2, pre_g, post_g, wg, wu, wd, *, tm):
    n, d = x2.shape
    f = wg.shape[1]
    return pl.pallas_call(
        _ffn_kernel,
        out_shape=jax.ShapeDtypeStruct((n, d), F32),
        grid=(n // tm,),
        in_specs=[
            pl.BlockSpec((tm, d), lambda i: (i, 0)),
            _const_spec((1, d)), _const_spec((1, d)),
            _const_spec((d, f)), _const_spec((d, f)), _const_spec((f, d)),
        ],
        out_specs=pl.BlockSpec((tm, d), lambda i: (i, 0)),
        compiler_params=pltpu.CompilerParams(
            dimension_semantics=("arbitrary",), vmem_limit_bytes=VMEM_LIMIT),
        name="ffn",
    )(x2, pre_g, post_g, wg, wu, wd)


def _mixin_kernel(x_ref, g_ref, w_ref, qfda_ref, qfn_ref,
                  daq, dak, davt, nq, nks, nkw, nvst, nvwt, nkc, nvc, gat, *, tk_da, tk_nsa):
    tm = x_ref.shape[0]
    t0 = pl.program_id(1) * tm
    h = _rms(x_ref[...], g_ref[...]).astype(BF)
    z = _dot(h, w_ref[...])

    lane = lax.broadcasted_iota(jnp.int32, (tm, LANES), 1)
    row = lax.broadcasted_iota(jnp.int32, (tm, LANES), 0) + t0
    pa = (row >> 6).astype(F32)
    pb = (row & 63).astype(F32)
    low = lane < HALF
    pab = jnp.where((lane & 1) == 0, pa, pb)
    pos_lo = jnp.where(lane < HALF, 0.0, jnp.where(lane < HALF + POS_FEATS, pab, 0.0))
    pos_hi = jnp.where(lane < POS_FEATS, pab, 0.0)
    blk_onehot = jnp.where(lane == (row >> 6), 1.0, 0.0)

    def ch(i):
        return z[:, i * LANES:(i + 1) * LANES]

    def put(ref, j, val):
        ref[:, j * LANES:(j + 1) * LANES] = val.astype(ref.dtype)

    for c in range(4):
        q = ch(c) * QK_SCALE_LOG2
        put(daq, 2 * c, jnp.where(low, q, qfda_ref[:, (2 * c) * LANES:(2 * c + 1) * LANES]))
        put(daq, 2 * c + 1, jnp.where(low, qfda_ref[:, (2 * c + 1) * LANES:(2 * c + 2) * LANES], q))
        k = ch(4 + c)
        put(dak, 2 * c, jnp.where(low, k, pos_lo))
        put(dak, 2 * c + 1, jnp.where(low, pos_hi, k))
        vt = ch(8 + c).T
        for s in range(tm // tk_da):
            davt[s, c * LANES:(c + 1) * LANES, :] = vt[:, s * tk_da:(s + 1) * tk_da].astype(BF)
        q = ch(12 + c) * QK_SCALE_LOG2
        put(nq, 2 * c, jnp.where(low, q, qfn_ref[:, (2 * c) * LANES:(2 * c + 1) * LANES]))
        put(nq, 2 * c + 1, jnp.where(low, qfn_ref[:, (2 * c + 1) * LANES:(2 * c + 2) * LANES], q))
    k = ch(16)
    put(nks, 0, jnp.where(low, k, pos_lo))
    put(nks, 1, blk_onehot)
    put(nks, 2, jnp.where(low, pos_hi, k))
    put(nks, 3, blk_onehot)
    k = ch(17)
    put(nkw, 0, jnp.where(low, k, pos_lo))
    put(nkw, 1, jnp.where(low, pos_hi, k))
    for src, ref in ((18, nvst), (19, nvwt)):
        vt = ch(src).T
        for s in range(tm // tk_nsa):
            ref[s] = vt[:, s * tk_nsa:(s + 1) * tk_nsa].astype(BF)
    nkc[...] = ch(20)
    nvc[...] = ch(21)
    gat[...] = jax.nn.sigmoid(ch(22))


def _mixin(x, g, w, qf_da, qf_n, *, tm, tk_da, tk_nsa):
    b, t, d = x.shape
    wcols = w.shape[1]
    assert tm % tk_da == 0 and tm % tk_nsa == 0
    row = lambda wd, dt: (jax.ShapeDtypeStruct((b, t, wd), dt),
                          pl.BlockSpec((None, tm, wd), lambda i, j: (i, j, 0)))
    tr = lambda rows, tk: (jax.ShapeDtypeStruct((b, t // tk, rows, tk), BF),
                           pl.BlockSpec((None, tm // tk, rows, tk), lambda i, j: (i, j, 0, 0)))
    outs = [row(1024, BF), row(1024, BF), tr(512, tk_da), row(1024, BF), row(512, BF), row(256, BF),
            tr(LANES, tk_nsa), tr(LANES, tk_nsa), row(LANES, F32), row(LANES, F32), row(LANES, F32)]
    return pl.pallas_call(
        functools.partial(_mixin_kernel, tk_da=tk_da, tk_nsa=tk_nsa),
        out_shape=[o[0] for o in outs],
        grid=(b, t // tm),
        in_specs=[
            pl.BlockSpec((None, tm, d), lambda i, j: (i, j, 0)),
            _const_spec((1, d)), _const_spec((d, wcols)),
            _const_spec((1, 1024)), _const_spec((1, 1024)),
        ],
        out_specs=[o[1] for o in outs],
        compiler_params=pltpu.CompilerParams(
            dimension_semantics=("arbitrary", "arbitrary"), vmem_limit_bytes=VMEM_LIMIT),
        name="mixin",
    )(x, g, w, qf_da, qf_n)


def _da_kernel(q_ref, k_ref, vt_ref, lq1, lk1, lq2, lk2, sg_ref, o_ref, s_ref, acc_ref, m_ref, l_ref,
               *, lam_init, tq, strip):
    qi = pl.program_id(2)
    m_ref[...] = jnp.full(m_ref.shape, NEG, F32)
    l_ref[...] = jnp.zeros(l_ref.shape, F32)
    acc_ref[...] = jnp.zeros(acc_ref.shape, F32)

    def scores(j, buf):
        ks = pl.multiple_of(j * tq, tq)
        for mp in range(2):
            lanes = slice(mp * LANES, (mp + 1) * LANES)
            s_ref[buf, mp] = _dot_nt(k_ref[pl.ds(ks, tq), lanes], q_ref[:, lanes])

    def consume(j, buf, diag=False):
        for mp in range(2):
            for c in range(tq // strip):
                sl = slice(c * strip, (c + 1) * strip)
                nk = (c + 1) * strip if diag else tq
                s = s_ref[buf, mp, 0:nk, sl]
                if diag:
                    kk = lax.broadcasted_iota(jnp.int32, (nk, strip), 0)
                    qq = lax.broadcasted_iota(jnp.int32, (nk, strip), 1) + c * strip
                    s = jnp.where(kk <= qq, s, NEG)
                m_ref[mp, :, sl], l_ref[mp, :, sl], acc_ref[mp, :, sl] = _softmax_step(
                    s, m_ref[mp, :, sl], l_ref[mp, :, sl], acc_ref[mp, :, sl],
                    vt_ref[j, :, 0:nk], jnp.ones((BF16_ROWS, nk), BF))

    scores(0, 0)

    def pair(p, carry):
        scores(2 * p + 1, 1)
        consume(2 * p, 0)
        scores(2 * p + 2, 0)
        consume(2 * p + 1, 1)
        return carry

    lax.fori_loop(0, qi // 2, pair, 0)

    @pl.when(qi % 2 == 1)
    def _():
        scores(qi, 1)
        consume(qi - 1, 0)
        consume(qi, 1, diag=True)

    @pl.when(qi % 2 == 0)
    def _():
        consume(qi, 0, diag=True)

    lam = (jnp.exp(jnp.sum(lq1[...] * lk1[...], axis=1, keepdims=True))
           - jnp.exp(jnp.sum(lq2[...] * lk2[...], axis=1, keepdims=True)) + lam_init)
    o = (acc_ref[0] / l_ref[0] - lam * (acc_ref[1] / l_ref[1])).T
    o_ref[...] = (_rms(o, sg_ref[...]) * (1.0 - lam_init)).astype(o_ref.dtype)


def _da_attention(daq, dak, davt, lq1, lk1, lq2, lk2, subln_g, *, lam_init, tq):
    b, t, _ = daq.shape
    nk = t // tq
    vec = _const_spec((1, DA_QK_DIM))
    return pl.pallas_call(
        functools.partial(_da_kernel, lam_init=lam_init, tq=tq, strip=2 * LANES),
        out_shape=jax.ShapeDtypeStruct((b, t, DA_HEADS * DA_V_DIM), BF),
        grid=(b, DA_HEADS, t // tq),
        in_specs=[
            pl.BlockSpec((None, tq, 2 * LANES), lambda i, h, j: (i, j, h)),
            pl.BlockSpec((None, t, 2 * LANES), lambda i, h, j: (i, 0, h)),
            pl.BlockSpec((None, nk, DA_V_DIM, tq), lambda i, h, j: (i, 0, h, 0)),
            vec, vec, vec, vec, _const_spec((1, DA_V_DIM)),
        ],
        out_specs=pl.BlockSpec((None, tq, DA_V_DIM), lambda i, h, j: (i, j, h)),
        scratch_shapes=[
            pltpu.VMEM((2, 2, tq, tq), F32),
            pltpu.VMEM((2, DA_V_DIM, tq), F32),
            pltpu.VMEM((2, 1, tq), F32),
            pltpu.VMEM((2, 1, tq), F32),
        ],
        compiler_params=pltpu.CompilerParams(
            dimension_semantics=("arbitrary", "arbitrary", "arbitrary"), vmem_limit_bytes=VMEM_LIMIT),
        name="diff_attn",
    )(daq, dak, davt, lq1, lk1, lq2, lk2, subln_g)


def _compress_kernel(xk_ref, xv_ref, pek_ref, pev_ref, w1k_ref, w1v_ref, w2k_ref, w2v_ref, cpos_ref,
                     kc_ref, vct_ref):
    ncp = xk_ref.shape[0]
    for x_ref, pe_ref, w1_ref, w2_ref, is_v in ((xk_ref, pek_ref, w1k_ref, w2k_ref, False),
                                                (xv_ref, pev_ref, w1v_ref, w2v_ref, True)):
        x = x_ref[...]
        xa = (x + pe_ref[0]).astype(BF)
        xb = (x + pe_ref[1]).astype(BF)
        for g in range(NSA_GROUPS):
            first = _dot(xa, w1_ref[0, g])
            second = _dot(xb, w1_ref[1, g])
            hid = first + pltpu.roll(second, ncp - 1, 0)
            act = (hid * jax.nn.sigmoid(hid)).astype(BF)
            out = _dot(act, w2_ref[g])
            if is_v:
                vct_ref[g] = out.T.astype(BF)
            else:
                kc_ref[g] = (out + cpos_ref[g]).astype(BF)


def _compress(xk, xv, pek, pev, w1k, w1v, w2k, w2v, cpos):
    b, ncp, width = xk.shape
    x_spec = pl.BlockSpec((None, ncp, width), lambda i: (i, 0, 0))
    return pl.pallas_call(
        _compress_kernel,
        out_shape=[jax.ShapeDtypeStruct((b, NSA_GROUPS, ncp, LANES), BF),
                   jax.ShapeDtypeStruct((b, NSA_GROUPS, LANES, ncp), BF)],
        grid=(b,),
        in_specs=[x_spec, x_spec,
                  _const_spec(pek.shape), _const_spec(pev.shape),
                  _const_spec(w1k.shape), _const_spec(w1v.shape),
                  _const_spec(w2k.shape), _const_spec(w2v.shape), _const_spec(cpos.shape)],
        out_specs=[pl.BlockSpec((None, NSA_GROUPS, ncp, LANES), lambda i: (i, 0, 0, 0)),
                   pl.BlockSpec((None, NSA_GROUPS, LANES, ncp), lambda i: (i, 0, 0, 0))],
        compiler_params=pltpu.CompilerParams(
            dimension_semantics=("arbitrary",), vmem_limit_bytes=VMEM_LIMIT),
        name="compress",
    )(xk, xv, pek, pev, w1k, w1v, w2k, w2v, cpos)


def _cmp_kernel(nq_ref, kc_ref, vct_ref, mt_ref, o_ref, sf_ref, ob_ref, *, tq, topk):
    ncp = kc_ref.shape[1]
    ns = mt_ref.shape[0]
    t0 = pl.program_id(1) * tq
    cidx = lax.broadcasted_iota(jnp.int32, (ncp, tq), 0)
    tpos = lax.broadcasted_iota(jnp.int32, (ncp, tq), 1) + t0
    valid = (cidx * CMP_STRIDE + (CMP_LEN - 1)) <= tpos
    col_live = (lax.broadcasted_iota(jnp.int32, (1, tq), 1) + t0) >= CMP_LEN - 1
    lane = lax.broadcasted_iota(jnp.int32, (tq, LANES), 1)
    mt = mt_ref[...]

    blk = lax.broadcasted_iota(jnp.int32, (ns, tq), 0)
    cur = (lax.broadcasted_iota(jnp.int32, (ns, tq), 1) + t0) >> 6
    sub8 = lax.broadcasted_iota(jnp.int32, (8, tq), 0)

    for g in range(NSA_GROUPS):
        kc = kc_ref[g]
        vct = vct_ref[g]
        imp = jnp.zeros((ns, tq), F32)
        for r in range(NSA_REP):
            c = 2 * r + g
            q = nq_ref[:, c * LANES:(c + 1) * LANES]
            s = jnp.where(valid, _dot_nt(kc, q), NEG)
            m = jnp.max(s, axis=0, keepdims=True)
            e = jnp.exp2(s - m)
            p = e * jnp.where(col_live, 1.0 / jnp.sum(e, axis=0, keepdims=True), 0.0)
            p_hi = p.astype(BF)
            imp = imp + _dot(mt, p_hi)
            o = _dot(vct, p_hi).T
            if g == 0:
                ob_ref[r] = o
            else:
                o_ref[:, r * LANES:(r + 1) * LANES] = jnp.where(lane < HALF, ob_ref[r], o).astype(o_ref.dtype)

        score = jnp.where(blk == 0, FORCE_SCORE,
                          jnp.where(blk == cur, FORCE_SCORE,
                                    jnp.where(blk == cur - 1, FORCE_SCORE,
                                              jnp.where(blk <= cur, imp, -1.0))))
        rows = [jnp.broadcast_to(score[i:i + 1, :], (8, tq)) for i in range(ns)]
        feats = []
        for kt in range(ns // 8):
            sc = score[8 * kt:8 * kt + 8, :]
            cnt = jnp.zeros((8, tq), F32)
            for i in range(ns):
                if i < 8 * kt:
                    beat = jnp.where(rows[i] >= sc, 1.0, 0.0)
                elif i >= 8 * kt + 8:
                    beat = jnp.where(rows[i] > sc, 1.0, 0.0)
                else:
                    beat = jnp.where(sub8 + 8 * kt > i,
                                     jnp.where(rows[i] >= sc, 1.0, 0.0),
                                     jnp.where(rows[i] > sc, 1.0, 0.0))
                cnt = cnt + beat
            feats.append(jnp.where(cnt < topk, 0.0, -SEL_MASK_BIG))
        feats.append(jnp.zeros((LANES - ns, tq), F32))
        sf_ref[g] = jnp.concatenate(feats, axis=0).T.astype(BF)


def _cmp_topk(nq, kc, vct, mt, *, tq, topk):
    b, t, _ = nq.shape
    ncp = kc.shape[2]
    return pl.pallas_call(
        functools.partial(_cmp_kernel, tq=tq, topk=topk),
        out_shape=[jax.ShapeDtypeStruct((b, t, NSA_REP * LANES), BF),
                   jax.ShapeDtypeStruct((b, NSA_GROUPS, t, LANES), BF)],
        grid=(b, t // tq),
        in_specs=[
            pl.BlockSpec((None, tq, 2 * NSA_REP * LANES), lambda i, j: (i, j, 0)),
            pl.BlockSpec((None, NSA_GROUPS, ncp, LANES), lambda i, j: (i, 0, 0, 0)),
            pl.BlockSpec((None, NSA_GROUPS, LANES, ncp), lambda i, j: (i, 0, 0, 0)),
            _const_spec(mt.shape),
        ],
        out_specs=[pl.BlockSpec((None, tq, NSA_REP * LANES), lambda i, j: (i, j, 0)),
                   pl.BlockSpec((None, NSA_GROUPS, tq, LANES), lambda i, j: (i, 0, j, 0))],
        scratch_shapes=[pltpu.VMEM((NSA_REP, tq, LANES), F32)],
        compiler_params=pltpu.CompilerParams(
            dimension_semantics=("arbitrary", "arbitrary"), vmem_limit_bytes=VMEM_LIMIT),
        name="cmp_topk",
    )(nq, kc, vct, mt)


def _nsa_flash_kernel(*refs, tq, window, selected):
    if selected:
        nq_ref, sf_ref, k_ref, vt_ref, o_ref, qs_ref, s_ref, acc_ref, m_ref, l_ref, ob_ref = refs
    else:
        nq_ref, k_ref, vt_ref, o_ref, qs_ref, s_ref, acc_ref, m_ref, l_ref, ob_ref = refs
    qi = pl.program_id(1)
    kd = qs_ref.shape[2]
    cols = NSA_REP * tq
    key = lax.broadcasted_iota(jnp.int32, (tq, tq), 0)
    qry = lax.broadcasted_iota(jnp.int32, (tq, tq), 1)
    ones = jnp.ones((BF16_ROWS, tq), BF)

    for g in range(NSA_GROUPS):
        for r in range(NSA_REP):
            c = 2 * r + g
            qs_ref[g, r * tq:(r + 1) * tq, 0:LANES] = nq_ref[:, c * LANES:(c + 1) * LANES]
            if selected:
                qs_ref[g, r * tq:(r + 1) * tq, LANES:2 * LANES] = sf_ref[g]

    m_ref[...] = jnp.full(m_ref.shape, NEG, F32)
    l_ref[...] = jnp.zeros(l_ref.shape, F32)
    acc_ref[...] = jnp.zeros(acc_ref.shape, F32)

    def scores(j, buf):
        ks = pl.multiple_of(j * tq, tq)
        for g in range(NSA_GROUPS):
            s_ref[buf, g] = _dot_nt(k_ref[pl.ds(ks, tq), g * kd:(g + 1) * kd], qs_ref[g])

    def consume(j, buf, mode, live=None):
        if mode == "edge":
            edge_qry = jnp.where(live, qry, tq)
        for g in range(NSA_GROUPS):
            vt = vt_ref[j, g * NSA_HEAD_DIM:(g + 1) * NSA_HEAD_DIM, :]
            for r in range(NSA_REP):
                sl = slice(r * tq, (r + 1) * tq)
                s = s_ref[buf, g, :, sl]
                if mode == "diag":
                    s = jnp.where(key <= qry, s, NEG)
                elif mode == "edge":
                    s = jnp.where(key > edge_qry, s, NEG)
                elif live is not None:
                    s = jnp.where(live, s, NEG)
                m_ref[g, :, sl], l_ref[g, :, sl], acc_ref[g, :, sl] = _softmax_step(
                    s, m_ref[g, :, sl], l_ref[g, :, sl], acc_ref[g, :, sl], vt, ones)

    if window:
        nback = WIN // tq
        for i in range(nback + 1):
            j = qi - nback + i
            scores(jnp.maximum(j, 0), i)
            if i == nback:
                consume(j, i, "diag")
            else:
                consume(jnp.maximum(j, 0), i, "edge" if i == 0 else "full", j >= 0)
    else:
        scores(0, 0)

        def pair(p, carry):
            scores(2 * p + 1, 1)
            consume(2 * p, 0, "full")
            scores(2 * p + 2, 0)
            consume(2 * p + 1, 1, "full")
            return carry

        lax.fori_loop(0, qi // 2, pair, 0)

        @pl.when(qi % 2 == 1)
        def _():
            scores(qi, 1)
            consume(qi - 1, 0, "full")
            consume(qi, 1, "diag")

        @pl.when(qi % 2 == 0)
        def _():
            consume(qi, 0, "diag")

    for g in range(NSA_GROUPS):
        o = acc_ref[g] / l_ref[g]
        for r in range(NSA_REP):
            ob_ref[r, g * NSA_HEAD_DIM:(g + 1) * NSA_HEAD_DIM, :] = o[:, r * tq:(r + 1) * tq]

    for r in range(NSA_REP):
        o_ref[:, r * LANES:(r + 1) * LANES] = ob_ref[r].T.astype(o_ref.dtype)


def _nsa_flash(nq, k, vt, sf=None, *, tq, window):
    b, t, _ = nq.shape
    selected = sf is not None
    kd = 2 * LANES if selected else LANES
    nk = t // tq
    assert vt.shape == (b, nk, LANES, tq) and k.shape == (b, t, NSA_GROUPS * kd)
    in_specs = [pl.BlockSpec((None, tq, 2 * NSA_REP * LANES), lambda i, j: (i, j, 0))]
    args = [nq]
    if selected:
        in_specs.append(pl.BlockSpec((None, NSA_GROUPS, tq, LANES), lambda i, j: (i, 0, j, 0)))
        args.append(sf)
    in_specs.append(pl.BlockSpec((None, t, NSA_GROUPS * kd), lambda i, j: (i, 0, 0)))
    args.append(k)
    in_specs.append(pl.BlockSpec((None, nk, LANES, tq), lambda i, j: (i, 0, 0, 0)))
    args.append(vt)
    cols = NSA_REP * tq
    return pl.pallas_call(
        functools.partial(_nsa_flash_kernel, tq=tq, window=window, selected=selected),
        out_shape=jax.ShapeDtypeStruct((b, t, NSA_REP * LANES), BF),
        grid=(b, t // tq),
        in_specs=in_specs,
        out_specs=pl.BlockSpec((None, tq, NSA_REP * LANES), lambda i, j: (i, j, 0)),
        scratch_shapes=[
            pltpu.VMEM((NSA_GROUPS, cols, kd), BF),
            pltpu.VMEM((WIN // tq + 1 if window else 2, NSA_GROUPS, tq, cols), F32),
            pltpu.VMEM((NSA_GROUPS, NSA_HEAD_DIM, cols), F32),
            pltpu.VMEM((NSA_GROUPS, 1, cols), F32),
            pltpu.VMEM((NSA_GROUPS, 1, cols), F32),
            pltpu.VMEM((NSA_REP, LANES, tq), F32),
        ],
        compiler_params=pltpu.CompilerParams(
            dimension_semantics=("arbitrary", "arbitrary"), vmem_limit_bytes=VMEM_LIMIT),
        name="nsa_window" if window else "nsa_selected",
    )(*args)


def _mixout_xa_kernel(x_ref, oa_ref, oc_ref, os_ref, ow_ref, gat_ref, e_ref, wa_ref, wb_ref, mpost_ref,
                      pre_ref, post_ref, wq_ref, wo_ref, k_ref, v_ref, o_ref, *, scale):
    gt = gat_ref[...]
    g_hi = gt.astype(BF)
    g_lo = (gt - g_hi.astype(F32)).astype(BF)
    g_split = jnp.concatenate([g_hi, g_lo], axis=1)
    ob = None
    for c, ref in enumerate((oc_ref, os_ref, ow_ref)):
        gate = _dot(g_split, e_ref[c])
        term = gate * ref[...].astype(F32)
        ob = term if ob is None else ob + term
    y = _dot(oa_ref[...], wa_ref[...]) + _dot(ob.astype(BF), wb_ref[...])
    x = x_ref[...] + _rms(y, mpost_ref[...])

    h = _rms(x, pre_ref[...]).astype(BF)
    q = (_dot(h, wq_ref[...]) * scale).astype(BF)
    hd = q.shape[1] // XA_HEADS
    outs = []
    for i in range(XA_HEADS):
        sl = slice(i * hd, (i + 1) * hd)
        s = _dot_nt(q[:, sl], k_ref[:, sl])
        m = jnp.max(s, axis=1, keepdims=True)
        e = jnp.exp(s - m)
        p = e / jnp.sum(e, axis=1, keepdims=True)
        outs.append(_dot(p.astype(BF), v_ref[:, sl]).astype(BF))
    y = _dot(jnp.concatenate(outs, axis=1), wo_ref[...])
    o_ref[...] = x + _rms(y, post_ref[...])


def _mixout_xa(x, oa, oc, osel, ow, gates, e, wa, wb, mix_post_g, pre_g, post_g, wq, wo, k, v, *, tm):
    b, t, d = x.shape
    nm = k.shape[1]
    hd = wq.shape[1] // XA_HEADS
    row = lambda w: pl.BlockSpec((None, tm, w), lambda i, j: (i, j, 0))
    kv = pl.BlockSpec((None, nm, k.shape[2]), lambda i, j: (i, 0, 0))
    vec = _const_spec((1, d))
    return pl.pallas_call(
        functools.partial(_mixout_xa_kernel, scale=float(hd) ** -0.5),
        out_shape=jax.ShapeDtypeStruct((b, t, d), F32),
        grid=(b, t // tm),
        in_specs=[row(d), row(oa.shape[2]), row(oc.shape[2]), row(osel.shape[2]), row(ow.shape[2]), row(LANES),
                  _const_spec(e.shape), _const_spec(wa.shape), _const_spec(wb.shape), vec,
                  vec, vec, _const_spec(wq.shape), _const_spec(wo.shape), kv, kv],
        out_specs=row(d),
        compiler_params=pltpu.CompilerParams(
            dimension_semantics=("arbitrary", "arbitrary"), vmem_limit_bytes=VMEM_LIMIT),
        name="mixout_xattn",
    )(x, oa, oc, osel, ow, gates, e, wa, wb, mix_post_g, pre_g, post_g, wq, wo, k, v)


def _memkv_kernel(m_ref, g_ref, wk_ref, wv_ref, k_ref, v_ref):
    m = _rms(m_ref[...], g_ref[...]).astype(BF)
    k_ref[...] = _dot(m, wk_ref[...]).astype(BF)
    v_ref[...] = _dot(m, wv_ref[...]).astype(BF)


def _memkv(mem, g, wk, wv):
    b, nm, d = mem.shape
    spec = pl.BlockSpec((None, nm, d), lambda i: (i, 0, 0))
    return pl.pallas_call(
        _memkv_kernel,
        out_shape=[jax.ShapeDtypeStruct((b, nm, d), BF)] * 2,
        grid=(b,),
        in_specs=[spec, _const_spec((1, d)), _const_spec(wk.shape), _const_spec(wv.shape)],
        out_specs=[spec, spec],
        compiler_params=pltpu.CompilerParams(
            dimension_semantics=("arbitrary",), vmem_limit_bytes=VMEM_LIMIT),
        name="mem_kv",
    )(mem, g, wk, wv)


def _bf16_terms(x, n):
    terms = []
    for _ in range(n):
        u = np.float32(x).view(np.uint32)
        hi = np.uint32((int(u) + 0x7FFF + ((int(u) >> 16) & 1)) & 0xFFFF0000).view(np.float32)
        terms.append(float(hi))
        x = float(np.float64(x) - np.float64(hi))
    return terms


def _query_feature_rows(slopes_by_chunk):
    out = np.zeros((1, len(slopes_by_chunk) * LANES), np.float32)
    for j, s in enumerate(slopes_by_chunk):
        base = j * LANES + (HALF if j % 2 == 0 else 0)
        c64 = _bf16_terms(s * 64.0 * LOG2E, POS_FEATS // 2)
        c1 = _bf16_terms(s * LOG2E, POS_FEATS // 2)
        for i in range(POS_FEATS // 2):
            out[0, base + 2 * i] = c64[i]
            out[0, base + 2 * i + 1] = c1[i]
    return jnp.asarray(out)


def _static_tables(t):
    ncp = t // CMP_STRIDE
    ns = t // SEL_LEN
    da = _alibi_slopes(DA_HEADS)
    qf_da = _query_feature_rows([da[j // 2] for j in range(2 * DA_HEADS)])
    nsa = _alibi_slopes(NSA_GROUPS * NSA_REP)
    qf_n = _query_feature_rows([nsa[(j % 2) * NSA_REP + j // 2] for j in range(2 * NSA_REP)])

    c_end = np.arange(ncp) * CMP_STRIDE + CMP_LEN - 1
    cpos = np.zeros((NSA_GROUPS, ncp, LANES), np.float32)
    for i in range(POS_FEATS // 2):
        cpos[0, :, HALF + 2 * i] = c_end // 64
        cpos[0, :, HALF + 2 * i + 1] = c_end % 64
        cpos[1, :, 2 * i] = c_end // 64
        cpos[1, :, 2 * i + 1] = c_end % 64

    c_start = np.arange(ncp) * CMP_STRIDE
    s_start = np.arange(ns) * SEL_LEN
    overlap = np.clip(np.minimum(c_start[:, None] + CMP_LEN, s_start[None, :] + SEL_LEN)
                      - np.maximum(c_start[:, None], s_start[None, :]), 0, None)
    mt = (overlap.astype(np.float32) / CMP_LEN).T
    mt[:, ncp - 1] = 0.0

    e = np.zeros((3, LANES, NSA_REP * LANES), np.float32)
    for g in range(NSA_GROUPS):
        for r in range(NSA_REP):
            for c in range(3):
                col = r * LANES + g * HALF
                e[c, g * NSA_REP * 3 + r * 3 + c, col:col + HALF] = 1.0
    e = np.concatenate([e, e], axis=1)
    return qf_da, qf_n, jnp.asarray(cpos), jnp.asarray(mt, BF), jnp.asarray(e, BF)


def _compress_weights(pe, w1, w2):
    hidden = w1.shape[1]
    w1r = w1.reshape(2, CMP_STRIDE, NSA_HEAD_DIM, hidden)
    w1g = jnp.zeros((2, NSA_GROUPS, CMP_STRIDE, NSA_GROUPS, NSA_HEAD_DIM, hidden), w1.dtype)
    for g in range(NSA_GROUPS):
        w1g = w1g.at[:, g, :, g].set(w1r)
    w1g = w1g.reshape(2, NSA_GROUPS, CMP_STRIDE * LANES, hidden).astype(BF)
    per = pe.reshape(2, CMP_STRIDE, 1, NSA_HEAD_DIM)
    peg = jnp.broadcast_to(per, (2, CMP_STRIDE, NSA_GROUPS, NSA_HEAD_DIM)).reshape(2, 1, CMP_STRIDE * LANES)
    w2g = jnp.zeros((NSA_GROUPS, hidden, NSA_GROUPS, NSA_HEAD_DIM), w2.dtype)
    for g in range(NSA_GROUPS):
        w2g = w2g.at[g, :, g].set(w2)
    w2g = w2g.reshape(NSA_GROUPS, hidden, LANES).astype(BF)
    return peg, w1g, w2g


def kernel(x, mem, ffn1_pre_g, ffn1_post_g, ffn1_w_gate, ffn1_w_up, ffn1_w_down, mix_pre_g, mix_post_g,
           w_mix_in, da_lambda_q1, da_lambda_k1, da_lambda_q2, da_lambda_k2, da_subln_g, cmp_k_pe,
           cmp_k_w1, cmp_k_w2, cmp_v_pe, cmp_v_w1, cmp_v_w2, w_mix_out, xa_pre_g, xa_post_g, mem_norm_g,
           xa_w_q, xa_w_k, xa_w_v, xa_w_o, ffn2_pre_g, ffn2_post_g, ffn2_w_gate, ffn2_w_up, ffn2_w_down):
    b, t, d = x.shape
    depth = ffn1_pre_g.shape[0]
    n = b * t
    tm = 512
    tq_da = 1024
    tq = 256
    tq_cmp = min(1024, t)
    ns = t // SEL_LEN
    assert t % tm == 0 and ns <= LANES and WIN % tq == 0
    assert d == DA_HEADS * DA_V_DIM + NSA_GROUPS * NSA_REP * NSA_HEAD_DIM
    qf_da, qf_n, cpos, mt, e = _static_tables(t)
    row = lambda v: v.reshape(1, -1)

    for l in range(depth):
        x2 = _ffn(x.reshape(n, d), row(ffn1_pre_g[l]), row(ffn1_post_g[l]),
                  ffn1_w_gate[l].astype(BF), ffn1_w_up[l].astype(BF), ffn1_w_down[l].astype(BF), tm=tm)

        w = w_mix_in[l]
        dq = DA_HEADS * 2 * DA_QK_DIM
        dv = DA_HEADS * DA_V_DIM
        nqw = NSA_GROUPS * NSA_REP * NSA_HEAD_DIM
        nkv = NSA_GROUPS * NSA_HEAD_DIM
        o0 = 2 * dq + dv
        w_nq = w[:, o0:o0 + nqw].reshape(d, NSA_GROUPS, NSA_REP, NSA_HEAD_DIM).transpose(0, 2, 1, 3).reshape(d, nqw)
        o1 = o0 + nqw
        seg = lambda i: w[:, o1 + i * nkv:o1 + (i + 1) * nkv]
        w_g = w[:, o1 + 6 * nkv:]
        w_g = jnp.pad(w_g, ((0, 0), (0, LANES - w_g.shape[1])))
        w_big = jnp.concatenate([w[:, :o0], w_nq, seg(2), seg(4), seg(3), seg(5), seg(0), seg(1), w_g],
                                axis=1).astype(BF)
        (daq, dak, davt, nq, nks, nkw, nvst, nvwt, nkc, nvc, gates) = _mixin(
            x2.reshape(b, t, d), row(mix_pre_g[l]), w_big, qf_da, qf_n, tm=2 * tm, tk_da=tq_da, tk_nsa=tq)

        lam_init = 0.8 - 0.6 * float(np.exp(-0.3 * l))
        o_a = _da_attention(daq, dak, davt, row(da_lambda_q1[l]), row(da_lambda_k1[l]),
                            row(da_lambda_q2[l]), row(da_lambda_k2[l]), row(da_subln_g[l]),
                            lam_init=lam_init, tq=tq_da)

        pek, w1k, w2k = _compress_weights(cmp_k_pe[l], cmp_k_w1[l], cmp_k_w2[l])
        pev, w1v, w2v = _compress_weights(cmp_v_pe[l], cmp_v_w1[l], cmp_v_w2[l])
        ncp = t // CMP_STRIDE
        kc, vct = _compress(nkc.reshape(b, ncp, CMP_STRIDE * LANES), nvc.reshape(b, ncp, CMP_STRIDE * LANES),
                            pek, pev, w1k, w1v, w2k, w2v, cpos)
        o_cmp, sf = _cmp_topk(nq, kc, vct, mt, tq=tq_cmp, topk=min(SEL_TOPK, ns))
        o_sel = _nsa_flash(nq, nks, nvst, sf, tq=tq, window=False)
        o_win = _nsa_flash(nq, nkw, nvwt, tq=tq, window=True)

        wo = w_mix_out[l]
        wa = wo[:dv].astype(BF)
        wb = wo[dv:].reshape(NSA_GROUPS, NSA_REP, NSA_HEAD_DIM, d).transpose(1, 0, 2, 3).reshape(nqw, d).astype(BF)
        mk, mv = _memkv(mem, row(mem_norm_g[l]), xa_w_k[l].astype(BF), xa_w_v[l].astype(BF))
        x4 = _mixout_xa(x2.reshape(b, t, d), o_a, o_cmp, o_sel, o_win, gates, e, wa, wb, row(mix_post_g[l]),
                        row(xa_pre_g[l]), row(xa_post_g[l]), xa_w_q[l].astype(BF), xa_w_o[l].astype(BF),
                        mk, mv, tm=2 * tm)

        x = _ffn(x4.reshape(n, d), row(ffn2_pre_g[l]), row(ffn2_post_g[l]),
                 ffn2_w_gate[l].astype(BF), ffn2_w_up[l].astype(BF), ffn2_w_down[l].astype(BF),
                 tm=tm).reshape(b, t, d)
    return x
```
